```python
import jax, jax.numpy as jnp
from jax import lax
import numpy as np

D_MODEL = 1024
BATCH = 32
SEQ = 2048
DEPTH = 2

CTX_LEN = 256
GRID_W = 64

RWKV_HEADS = 4
RWKV_HEAD = 64
RWKV_DIM = RWKV_HEADS * RWKV_HEAD
DECAY_LORA = 64
AAA_LORA = 64
MV_LORA = 32
GATE_LORA = 128
LNX_EPS = 64e-5
ATT_HEADS = 8
ATT_KV_HEADS = 2
ATT_GROUP = ATT_HEADS // ATT_KV_HEADS
ATT_HEAD = 64
ATT_Q_DIM = ATT_HEADS * ATT_HEAD
ATT_KV_DIM = ATT_KV_HEADS * ATT_HEAD
ATT_SCALE = ATT_HEAD ** -0.5
WINDOW = 128
BLOCK = 128
ROPE_BASE = 10000.0
ROPE_FREQS = ATT_HEAD // 4
NEG_INF = -1e30
FOURIER_GROUPS = 4
FOURIER_GROUP_DIM = 64
FOURIER_DIM = FOURIER_GROUPS * FOURIER_GROUP_DIM
N_BRANCH = 3
D_FF = 2816
EPS = 1e-6

RWKV_COL_SIZES = (RWKV_DIM, RWKV_DIM, RWKV_DIM, DECAY_LORA, DECAY_LORA, AAA_LORA, AAA_LORA, GATE_LORA)
RWKV_COLS = sum(RWKV_COL_SIZES)
IN_SIZES = (RWKV_COLS, ATT_Q_DIM, ATT_KV_DIM, ATT_KV_DIM, FOURIER_DIM, N_BRANCH * D_MODEL)
IN_COLS = sum(IN_SIZES)

kernel_name = "hybrid_rwkv7_swa_fourier_diffusion_block"


def _split(t, sizes):
    return jnp.split(t, [int(i) for i in np.cumsum(sizes)[:-1]], axis=-1)


def rms_norm(x, g):
    xf = x.astype(jnp.float32)
    y = xf * lax.rsqrt(jnp.mean(xf * xf, axis=-1, keepdims=True) + EPS)
    return (y * g.astype(jnp.float32)).astype(x.dtype)


def modulate(x, shift, scale):
    return x * (1 + scale) + shift


def _prev(x):
    return jnp.pad(x[:, :-1], ((0, 0), (1, 0), (0, 0)))


def _next(x):
    return jnp.pad(x[:, 1:], ((0, 0), (0, 1), (0, 0)))


def _heads(t):
    return t.astype(jnp.float32).reshape(t.shape[0], t.shape[1], RWKV_HEADS, RWKV_HEAD)


def rwkv_inputs(u, mu, w0, w2, a0, a2, g2, k_k, k_a, v_first, vres):
    u = u + mu[0] * (_prev(u) - u) + mu[1] * (_next(u) - u)
    r, k, v, xw_f, xw_b, xa_f, xa_b, xg = _split(u, RWKV_COL_SIZES)
    if vres is None:
        v_first = v
    else:
        v0, v1, v2 = vres
        v = v + (v_first - v) * jax.nn.sigmoid(v0 + (v @ v1) @ v2)
    kk = _heads(k * k_k)
    kk = kk * lax.rsqrt(jnp.maximum(jnp.sum(kk * kk, axis=-1, keepdims=True), 1e-24))
    decays, keys, removals = [], [], []
    for d, (xw, xa) in enumerate(((xw_f, xa_f), (xw_b, xa_b))):
        w_log = -jax.nn.softplus(-(w0[d] + jnp.tanh(xw) @ w2[d]).astype(jnp.float32)) - 0.5
        decays.append(_heads(jnp.exp(-jnp.exp(w_log))))
        a = jax.nn.sigmoid(a0[d] + xa @ a2[d])
        keys.append(_heads(k * (1 + (a - 1) * k_a)))
        removals.append(kk * _heads(a))
    g = jax.nn.sigmoid(xg) @ g2
    feat = (_heads(r), _heads(v), kk, tuple(decays), tuple(keys), tuple(removals), g)
    return feat, v_first


def wkv_scan(state0, feat, d, reverse):
    r, v, kk, decays, keys, removals, _ = feat

    def step(S, inp):
        r_t, w_t, k_t, v_t, kk_t, b_t = inp
        sa = jnp.einsum("bhvk,bhk->bhv", S, -kk_t)
        S = S * w_t[:, :, None, :] + sa[..., :, None] * b_t[..., None, :] + v_t[..., :, None] * k_t[..., None, :]
        return S, jnp.einsum("bhvk,bhk->bhv", S, r_t)

    xs = tuple(jnp.swapaxes(t, 0, 1) for t in (r, decays[d], keys[d], v, kk, removals[d]))
    S_last, y = lax.scan(step, state0, xs, reverse=reverse)
    return jnp.swapaxes(y, 0, 1), S_last


def rwkv_readout(y, feat, r_k, lnx_w, lnx_b):
    r, v, kk, decays, keys, removals, g = feat
    B, T = y.shape[:2]
    mean = jnp.mean(y, axis=-1, keepdims=True)
    var = jnp.mean(jnp.square(y - mean), axis=-1, keepdims=True)
    yn = ((y - mean) * lax.rsqrt(var + LNX_EPS)).reshape(B, T, RWKV_DIM) * lnx_w + lnx_b
    k_bar = 0.5 * (keys[0] + keys[1])
    bonus = jnp.sum(r * k_bar * r_k.reshape(RWKV_HEADS, RWKV_HEAD), axis=-1, keepdims=True) * v
    return (yn + bonus.reshape(B, T, RWKV_DIM)) * g


def rwkv_branch(u_ctx, u_lat, vf_ctx, vf_lat, vres, need_ctx, mu, w0, w2, a0, a2, g2, k_k, k_a, r_k, lnx_w, lnx_b):
    fc, vf_ctx = rwkv_inputs(u_ctx, mu, w0, w2, a0, a2, g2, k_k, k_a, vf_ctx, vres)
    fl, vf_lat = rwkv_inputs(u_lat, mu, w0, w2, a0, a2, g2, k_k, k_a, vf_lat, vres)
    S0 = jnp.zeros((u_lat.shape[0], RWKV_HEADS, RWKV_HEAD, RWKV_HEAD), jnp.float32)
    yc_f, Sc_f = wkv_scan(S0, fc, 0, reverse=False)
    yc_b, Sc_b = wkv_scan(S0, fc, 1, reverse=True)
    yl_f, _ = wkv_scan(Sc_f, fl, 0, reverse=False)
    yl_b, _ = wkv_scan(Sc_b, fl, 1, reverse=True)
    out_lat = rwkv_readout(yl_f + yl_b, fl, r_k, lnx_w, lnx_b)
    out_ctx = rwkv_readout(yc_f + yc_b, fc, r_k, lnx_w, lnx_b) if need_ctx else None
    return out_ctx, out_lat, vf_ctx, vf_lat


def axial_rope_tables(rows):
    row_id = jnp.repeat(jnp.arange(rows), GRID_W).astype(jnp.float32)
    col_id = jnp.tile(jnp.arange(GRID_W), rows).astype(jnp.float32)
    inv = ROPE_BASE ** (-jnp.arange(ROPE_FREQS, dtype=jnp.float32) / ROPE_FREQS)
    ang = jnp.stack([row_id[:, None] * inv, col_id[:, None] * inv], axis=1)
    return jnp.cos(ang), jnp.sin(ang)


def apply_axial_rope(x, cos, sin):
    B, T, H, _ = x.shape
    xr = x.astype(jnp.float32).reshape(B, T, H, 2, 2, ROPE_FREQS)
    x1, x2 = xr[..., 0, :], xr[..., 1, :]
    c, s = cos[:, None], sin[:, None]
    out = jnp.stack([x1 * c - x2 * s, x2 * c + x1 * s], axis=-2)
    return out.reshape(x.shape).astype(x.dtype)


def _sink_col(sink, shape):
    return jnp.broadcast_to(sink.astype(jnp.float32).reshape(1, ATT_KV_HEADS, ATT_GROUP, 1, 1), shape)


def context_attention(q, k, v, sink):
    s = jnp.einsum("bqhgd,bkhd->bhgqk", q, k).astype(jnp.float32) * ATT_SCALE
    logits = jnp.concatenate([s, _sink_col(sink, s.shape[:-1] + (1,))], axis=-1)
    p = jax.nn.softmax(logits, axis=-1)[..., :-1].astype(v.dtype)
    o = jnp.einsum("bhgqk,bkhd->bqhgd", p, v)
    return o.reshape(o.shape[0], o.shape[1], ATT_Q_DIM)


def windowed_attention(q, k, v, k_ctx, v_ctx, sink):
    B, T = q.shape[:2]
    n_ctx = k_ctx.shape[1]
    kp = jnp.pad(k, ((0, 0), (BLOCK, BLOCK), (0, 0), (0, 0)))
    vp = jnp.pad(v, ((0, 0), (BLOCK, BLOCK), (0, 0), (0, 0)))

    def block(i):
        start = i * BLOCK
        q_b = lax.dynamic_slice_in_dim(q, start, BLOCK, axis=1)
        k_b = lax.dynamic_slice_in_dim(kp, start, 3 * BLOCK, axis=1)
        v_b = lax.dynamic_slice_in_dim(vp, start, 3 * BLOCK, axis=1)
        q_pos = start + jnp.arange(BLOCK)
        k_pos = start - BLOCK + jnp.arange(3 * BLOCK)
        valid = (jnp.abs(q_pos[:, None] - k_pos[None, :]) <= WINDOW) & (k_pos[None, :] >= 0) & (k_pos[None, :] < T)
        s_w = jnp.einsum("bqhgd,bkhd->bhgqk", q_b, k_b).astype(jnp.float32) * ATT_SCALE
        s_w = jnp.where(valid, s_w, NEG_INF)
        s_c = jnp.einsum("bqhgd,bkhd->bhgqk", q_b, k_ctx).astype(jnp.float32) * ATT_SCALE
        logits = jnp.concatenate([s_c, s_w, _sink_col(sink, s_c.shape[:-1] + (1,))], axis=-1)
        p = jax.nn.softmax(logits, axis=-1).astype(v.dtype)
        p_c, p_w = p[..., :n_ctx], p[..., n_ctx:n_ctx + 3 * BLOCK]
        return jnp.einsum("bhgqk,bkhd->bqhgd", p_c, v_ctx) + jnp.einsum("bhgqk,bkhd->bqhgd", p_w, v_b)

    o = lax.map(block, jnp.arange(T // BLOCK))
    return jnp.moveaxis(o, 0, 1).reshape(B, T, ATT_Q_DIM)


def fourier_mix(u):
    B, T, _ = u.shape
    z = u.astype(jnp.float32).reshape(B, T, FOURIER_GROUPS, FOURIER_GROUP_DIM)
    return jnp.fft.fftn(z, axes=(1, 3), norm="ortho").real.reshape(B, T, FOURIER_DIM).astype(u.dtype)


def merge_branches(y_r, y_a, y_f, gate_logits, wb_r, wb_a, wb_f, w_o):
    g_r, g_a, g_f = jnp.split(jax.nn.sigmoid(gate_logits), N_BRANCH, axis=-1)
    return (g_r * (y_r @ wb_r) + g_a * (y_a @ wb_a) + g_f * (y_f @ wb_f)) @ w_o


def conv_ffn(f, up, conv_w, conv_b, down):
    z_gate, z_val = jnp.split(f @ up, 2, axis=-1)
    z_gate = conv_w[0] * _prev(z_gate) + conv_w[1] * z_gate + conv_w[2] * _next(z_gate) + conv_b
    return (jax.nn.gelu(z_gate, approximate=True) * z_val) @ down


def setup_inputs(seed: int = 0) -> dict:
    key = jax.random.key(seed)
    ks = iter(jax.random.split(key, 48))
    D, L = D_MODEL, DEPTH

    def nrm(shape, scale):
        return scale * jax.random.normal(next(ks), shape, jnp.float32)

    def unif(shape, lo, hi):
        return jax.random.uniform(next(ks), shape, jnp.float32, lo, hi)

    return {
        "x": nrm((BATCH, SEQ, D), 1.0),
        "c": nrm((BATCH, D), 1.0),
        "ctx": nrm((BATCH, CTX_LEN, D), 1.0),
        "c_ctx": nrm((D,), 1.0),
        "mod_w": nrm((L, D, 6 * D), 0.5 * D ** -0.5),
        "mod_b": nrm((L, 6 * D), 0.02),
        "norm_mix_pre": 1.0 + nrm((L, D), 0.02),
        "norm_mix_post": 1.0 + nrm((L, D), 0.02),
        "norm_ffn_pre": 1.0 + nrm((L, D), 0.02),
        "norm_ffn_post": 1.0 + nrm((L, D), 0.02),
        "w_in": nrm((L, D, IN_COLS), D ** -0.5),
        "rwkv_mu": unif((L, 2, RWKV_COLS), 0.0, 0.5),
        "rwkv_w0": unif((L, 2, RWKV_DIM), -5.0, 0.0),
        "rwkv_w2": nrm((L, 2, DECAY_LORA, RWKV_DIM), 0.1 * DECAY_LORA ** -0.5),
        "rwkv_a0": nrm((L, 2, RWKV_DIM), 0.1),
        "rwkv_a2": nrm((L, 2, AAA_LORA, RWKV_DIM), 0.5 * AAA_LORA ** -0.5),
        "rwkv_g2": nrm((L, GATE_LORA, RWKV_DIM), GATE_LORA ** -0.5),
        "rwkv_k_k": 0.85 + nrm((L, RWKV_DIM), 0.02),
        "rwkv_k_a": 1.0 + nrm((L, RWKV_DIM), 0.02),
        "rwkv_r_k": nrm((L, RWKV_DIM), 0.1),
        "rwkv_lnx_w": 1.0 + nrm((L, RWKV_DIM), 0.02),
        "rwkv_lnx_b": nrm((L, RWKV_DIM), 0.02),
        "rwkv_v0": nrm((L - 1, RWKV_DIM), 0.1),
        "rwkv_v1": nrm((L - 1, RWKV_DIM, MV_LORA), RWKV_DIM ** -0.5),
        "rwkv_v2": nrm((L - 1, MV_LORA, RWKV_DIM), 0.5 * MV_LORA ** -0.5),
        "attn_sink": nrm((L, ATT_HEADS), 0.5),
        "w_branch_rwkv": nrm((L, RWKV_DIM, D), RWKV_DIM ** -0.5),
        "w_branch_attn": nrm((L, ATT_Q_DIM, D), ATT_Q_DIM ** -0.5),
        "w_branch_fourier": nrm((L, FOURIER_DIM, D), FOURIER_DIM ** -0.5),
        "w_out": nrm((L, D, D), D ** -0.5),
        "ffn_up": nrm((L, D, 2 * D_FF), D ** -0.5),
        "ffn_conv_w": nrm((L, 3, D_FF), 0.5),
        "ffn_conv_b": nrm((L, D_FF), 0.02),
        "ffn_down": nrm((L, D_FF, D), D_FF ** -0.5),
    }


def reference(x, c, ctx, c_ctx, mod_w, mod_b, norm_mix_pre, norm_mix_post, norm_ffn_pre, norm_ffn_post,
              w_in, rwkv_mu, rwkv_w0, rwkv_w2, rwkv_a0, rwkv_a2, rwkv_g2, rwkv_k_k, rwkv_k_a, rwkv_r_k,
              rwkv_lnx_w, rwkv_lnx_b, rwkv_v0, rwkv_v1, rwkv_v2, attn_sink, w_branch_rwkv, w_branch_attn,
              w_branch_fourier, w_out, ffn_up, ffn_conv_w, ffn_conv_b, ffn_down):
    B, T, _ = x.shape
    n_ctx = ctx.shape[1]
    rows = T // GRID_W
    cos, sin = axial_rope_tables(rows)
    s_lat = jax.nn.silu(c)[:, None, :]
    s_ctx = jax.nn.silu(c_ctx)[None, None, :]
    h, hc = x, ctx
    vf_lat = vf_ctx = None
    for l in range(DEPTH):
        need_ctx = l < DEPTH - 1
        m = jnp.split(s_lat @ mod_w[l] + mod_b[l], 6, axis=-1)
        mc = jnp.split(s_ctx @ mod_w[l] + mod_b[l], 6, axis=-1)
        a = modulate(rms_norm(h, norm_mix_pre[l]), m[0], m[1])
        ac = modulate(rms_norm(hc, norm_mix_pre[l]), mc[0], mc[1])
        u_r, u_q, u_k, u_v, u_f, u_g = _split(a @ w_in[l], IN_SIZES)
        c_r, c_q, c_k, c_v, c_f, c_g = _split(ac @ w_in[l], IN_SIZES)
        vres = None if l == 0 else (rwkv_v0[l - 1], rwkv_v1[l - 1], rwkv_v2[l - 1])
        yr_c, yr, vf_ctx, vf_lat = rwkv_branch(
            c_r, u_r, vf_ctx, vf_lat, vres, need_ctx, rwkv_mu[l], rwkv_w0[l], rwkv_w2[l], rwkv_a0[l],
            rwkv_a2[l], rwkv_g2[l], rwkv_k_k[l], rwkv_k_a[l], rwkv_r_k[l], rwkv_lnx_w[l], rwkv_lnx_b[l])
        q = apply_axial_rope(u_q.reshape(B, T, ATT_HEADS, ATT_HEAD), cos, sin)
        q = q.reshape(B, T, ATT_KV_HEADS, ATT_GROUP, ATT_HEAD)
        k = apply_axial_rope(u_k.reshape(B, T, ATT_KV_HEADS, ATT_HEAD), cos, sin)
        v = u_v.reshape(B, T, ATT_KV_HEADS, ATT_HEAD)
        k_c = c_k.reshape(B, n_ctx, ATT_KV_HEADS, ATT_HEAD)
        v_c = c_v.reshape(B, n_ctx, ATT_KV_HEADS, ATT_HEAD)
        ya = windowed_attention(q, k, v, k_c, v_c, attn_sink[l])
        mix = merge_branches(yr, ya, fourier_mix(u_f), u_g, w_branch_rwkv[l], w_branch_attn[l],
                             w_branch_fourier[l], w_out[l])
        h = h + m[2] * rms_norm(mix, norm_mix_post[l])
        if need_ctx:
            ya_c = context_attention(c_q.reshape(B, n_ctx, ATT_KV_HEADS, ATT_GROUP, ATT_HEAD), k_c, v_c, attn_sink[l])
            mix_c = merge_branches(yr_c, ya_c, fourier_mix(c_f), c_g, w_branch_rwkv[l], w_branch_attn[l],
                                   w_branch_fourier[l], w_out[l])
            hc = hc + mc[2] * rms_norm(mix_c, norm_mix_post[l])
        f = modulate(rms_norm(h, norm_ffn_pre[l]), m[3], m[4])
        h = h + m[5] * rms_norm(conv_ffn(f, ffn_up[l], ffn_conv_w[l], ffn_conv_b[l], ffn_down[l]), norm_ffn_post[l])
        if need_ctx:
            fc = modulate(rms_norm(hc, norm_ffn_pre[l]), mc[3], mc[4])
            hc = hc + mc[5] * rms_norm(conv_ffn(fc, ffn_up[l], ffn_conv_w[l], ffn_conv_b[l], ffn_down[l]), norm_ffn_post[l])
    return h
```

```python
import functools
import math

import numpy as np
import jax
import jax.numpy as jnp
from jax import lax
from jax.experimental import pallas as pl
from jax.experimental.pallas import tpu as pltpu

F32 = jnp.float32
BF16 = jnp.bfloat16

GRID_W = 64
RWKV_HEADS = 4
RWKV_HEAD = 64
RWKV_DIM = RWKV_HEADS * RWKV_HEAD
DECAY_LORA = 64
AAA_LORA = 64
MV_LORA = 32
GATE_LORA = 128
LNX_EPS = 64e-5
ATT_HEADS = 8
ATT_KV_HEADS = 2
ATT_GROUP = ATT_HEADS // ATT_KV_HEADS
ATT_HEAD = 64
ATT_Q_DIM = ATT_HEADS * ATT_HEAD
ATT_KV_DIM = ATT_KV_HEADS * ATT_HEAD
ATT_SCALE = ATT_HEAD ** -0.5
WINDOW = 128
ROPE_BASE = 10000.0
ROPE_FREQS = ATT_HEAD // 4
NEG_INF = -1e30
FOURIER_GROUPS = 4
FOURIER_GROUP_DIM = 64
FOURIER_DIM = FOURIER_GROUPS * FOURIER_GROUP_DIM
EPS = 1e-6
RWKV_COLS = 3 * RWKV_DIM + 2 * DECAY_LORA + 2 * AAA_LORA + GATE_LORA

ROW_TILE = 256
HALO = 8
WKV_CHUNK = 64
ATT_BLOCK = 128
FF_CHUNK = 256
VMEM_LIMIT = 56 * 1024 * 1024


def _cparams(n_axes):
    return pltpu.CompilerParams(dimension_semantics=("arbitrary",) * n_axes,
                                vmem_limit_bytes=VMEM_LIMIT)


def _dot(a, b):
    return jnp.dot(a.astype(BF16), b.astype(BF16), preferred_element_type=F32)


def _dot_nt(a, b):
    return lax.dot_general(a.astype(BF16), b.astype(BF16), (((1,), (1,)), ((), ())),
                           preferred_element_type=F32)


def _split3(x):
    hi = x.astype(BF16)
    r1 = x - hi.astype(F32)
    mid = r1.astype(BF16)
    lo = (r1 - mid.astype(F32)).astype(BF16)
    return hi, mid, lo


def _dot_exact_rhs(x, m):
    hi, mid, lo = _split3(x)
    return (jnp.dot(hi, m, preferred_element_type=F32) + jnp.dot(mid, m, preferred_element_type=F32)
            + jnp.dot(lo, m, preferred_element_type=F32))


def _dot_exact_lhs(m, x):
    hi, mid, lo = _split3(x)
    return (jnp.dot(m, hi, preferred_element_type=F32) + jnp.dot(m, mid, preferred_element_type=F32)
            + jnp.dot(m, lo, preferred_element_type=F32))


def _sigmoid(x):
    return 1.0 / (1.0 + jnp.exp(-x))


def _const_spec(shape):
    nd = len(shape)
    return pl.BlockSpec(shape, lambda *_: (0,) * nd)


def _rms_mod(x, g, shift, scale):
    y = x * lax.rsqrt(jnp.mean(x * x, axis=-1, keepdims=True) + EPS) * g
    return y * (1.0 + scale) + shift


def _mod_kernel(c_ref, w_ref, b_ref, o_ref):
    x = c_ref[...]
    o_ref[0] = _dot(x * _sigmoid(x), w_ref[0]) + b_ref[0]


def _modulation(cvec, mod_w, mod_b):
    depth, d, n = mod_w.shape
    rows = cvec.shape[0]
    tn = 1536
    return pl.pallas_call(
        _mod_kernel,
        grid=(depth, n // tn),
        in_specs=[pl.BlockSpec((rows, d), lambda l, j: (0, 0)),
                  pl.BlockSpec((1, d, tn), lambda l, j: (l, 0, j)),
                  pl.BlockSpec((1, 1, tn), lambda l, j: (l, 0, j))],
        out_specs=pl.BlockSpec((1, rows, tn), lambda l, j: (l, 0, j)),
        out_shape=jax.ShapeDtypeStruct((depth, rows, n), F32),
        compiler_params=_cparams(2),
        name="adaln_modulation",
    )(cvec, mod_w.astype(BF16), mod_b.reshape(depth, 1, n))


def _rope(x, cos, sin_signed):
    n = x.shape[1]
    lane = lax.broadcasted_iota(jnp.int32, x.shape, 1)
    first = (lane & ROPE_FREQS) == 0
    partner = jnp.where(first, pltpu.roll(x, n - ROPE_FREQS, 1), pltpu.roll(x, ROPE_FREQS, 1))
    return x * cos + partner * sin_signed


def _inproj_kernel(h_ref, mod_ref, g_ref, wr_ref, wq_ref, wkv_ref, wf_ref, wg_ref, dft_ref, cos_ref, sin_ref,
                   ur_ref, q_ref, kv_ref, z_ref, ug_ref):
    a = _rms_mod(h_ref[0], g_ref[...], mod_ref[0, 0, 0:1, :], mod_ref[0, 0, 1:2, :]).astype(BF16)
    ur_ref[0] = jnp.dot(a, wr_ref[...], preferred_element_type=F32)
    cos = cos_ref[...]
    sin = sin_ref[...]
    q = jnp.dot(a, wq_ref[...], preferred_element_type=F32)
    q_ref[0] = (_rope(q, cos, sin) * ATT_SCALE).astype(BF16)
    kv = jnp.dot(a, wkv_ref[...], preferred_element_type=F32)
    k = _rope(kv[:, :ATT_KV_DIM], cos[:, :ATT_KV_DIM], sin[:, :ATT_KV_DIM])
    kv_ref[0, :, :ATT_KV_DIM] = k.astype(BF16)
    kv_ref[0, :, ATT_KV_DIM:] = kv[:, ATT_KV_DIM:].astype(BF16)
    uf = jnp.dot(a, wf_ref[...], preferred_element_type=F32)
    z_ref[0] = _dot(uf, dft_ref[...]).astype(BF16)
    ug_ref[0] = jnp.dot(a, wg_ref[...], preferred_element_type=F32).astype(BF16)


def _inproj(h, modtab, g, w_in, dft_c, cos_t, sin_t, n_lat_tiles):
    b, l, d = h.shape
    tm = ROW_TILE
    n_g = w_in.shape[1] - (RWKV_COLS + ATT_Q_DIM + 2 * ATT_KV_DIM + FOURIER_DIM)
    o = np.cumsum([0, RWKV_COLS, ATT_Q_DIM, 2 * ATT_KV_DIM, FOURIER_DIM, n_g])
    wb = w_in.astype(BF16)
    ws = [wb[:, o[i]:o[i + 1]] for i in range(5)]
    row = lambda bb, i: (bb, i, 0)
    outs = [(RWKV_COLS, F32), (ATT_Q_DIM, BF16), (2 * ATT_KV_DIM, BF16), (2 * FOURIER_DIM, BF16), (n_g, BF16)]
    return pl.pallas_call(
        _inproj_kernel,
        grid=(b, l // tm),
        in_specs=[pl.BlockSpec((1, tm, d), row),
                  pl.BlockSpec((1, 1, 6, d), lambda bb, i: (bb, jnp.where(i < n_lat_tiles, 0, 1), 0, 0)),
                  _const_spec((1, d))] + [_const_spec(w.shape) for w in ws] + [
                  _const_spec(dft_c.shape),
                  pl.BlockSpec((tm, ATT_Q_DIM), lambda bb, i: (i, 0)),
                  pl.BlockSpec((tm, ATT_Q_DIM), lambda bb, i: (i, 0))],
        out_specs=[pl.BlockSpec((1, tm, n), row) for n, _ in outs],
        out_shape=[jax.ShapeDtypeStruct((b, l, n), dt) for n, dt in outs],
        compiler_params=_cparams(2),
        name="in_projection",
    )(h, modtab, g.reshape(1, d), *ws, dft_c, cos_t, sin_t)


def _feat_kernel(*refs, has_vres, n_lat_tiles, n_tiles):
    if has_vres:
        (u_ref, up_ref, un_ref, vf_ref, mu_ref, vec_ref, w0_ref, a0_ref, w2_ref, a2_ref, g2_ref, seg_ref,
         v0_ref, v1_ref, v2_ref, fc_ref, ff_ref, fb_ref, ro_ref, buf_ref) = refs
    else:
        (u_ref, up_ref, un_ref, mu_ref, vec_ref, w0_ref, a0_ref, w2_ref, a2_ref, g2_ref, seg_ref,
         fc_ref, ff_ref, fb_ref, ro_ref, buf_ref) = refs
    i = pl.program_id(1)
    tm = u_ref.shape[1]
    prev_ok = jnp.logical_and(i != 0, i != n_lat_tiles)
    next_ok = jnp.logical_and(i != n_lat_tiles - 1, i != n_tiles - 1)
    u = u_ref[0]
    buf_ref[0:HALO, :] = jnp.where(prev_ok, up_ref[0], 0.0)
    buf_ref[HALO:HALO + tm, :] = u
    buf_ref[HALO + tm:, :] = jnp.where(next_ok, un_ref[0], 0.0)
    u_prev = buf_ref[HALO - 1:HALO - 1 + tm, :]
    u_next = buf_ref[HALO + 1:HALO + 1 + tm, :]
    us = u + mu_ref[0:1, :] * (u_prev - u) + mu_ref[1:2, :] * (u_next - u)

    n = RWKV_DIM
    r = us[:, 0:n]
    k = us[:, n:2 * n]
    v = us[:, 2 * n:3 * n]
    o = 3 * n
    xw = (us[:, o:o + DECAY_LORA], us[:, o + DECAY_LORA:o + 2 * DECAY_LORA])
    o += 2 * DECAY_LORA
    xa = (us[:, o:o + AAA_LORA], us[:, o + AAA_LORA:o + 2 * AAA_LORA])
    o += 2 * AAA_LORA
    xg = us[:, o:o + GATE_LORA]
    k_k = vec_ref[0:1, :]
    k_a = vec_ref[1:2, :]
    r_k = vec_ref[2:3, :]
    seg = seg_ref[...]

    if has_vres:
        gate = _sigmoid(v0_ref[...] + _dot(_dot(v, v1_ref[...]), v2_ref[...]))
        v = v + (vf_ref[0] - v) * gate
    kk = k * k_k
    kk = kk * lax.rsqrt(jnp.maximum(_dot_exact_rhs(kk * kk, seg), 1e-24))
    keys = []
    for d, out_ref in enumerate((ff_ref, fb_ref)):
        z = w0_ref[d:d + 1, :] + _dot(jnp.tanh(xw[d]), w2_ref[d])
        softplus = jnp.maximum(-z, 0.0) + jnp.log(1.0 + jnp.exp(-jnp.abs(z)))
        out_ref[0, :, 0:n] = -jnp.exp(-softplus - 0.5)
        a = _sigmoid(a0_ref[d:d + 1, :] + _dot(xa[d], a2_ref[d]))
        key = k * (1.0 + (a - 1.0) * k_a)
        keys.append(key)
        out_ref[0, :, n:2 * n] = key
        out_ref[0, :, 2 * n:3 * n] = kk * a
    g = _dot(_sigmoid(xg), g2_ref[...])
    bonus = _dot_exact_rhs(r * (0.5 * (keys[0] + keys[1])) * r_k, seg) * v
    fc_ref[0, :, 0:n] = r
    fc_ref[0, :, n:2 * n] = v
    fc_ref[0, :, 2 * n:3 * n] = kk
    ro_ref[0, :, 0:n] = bonus
    ro_ref[0, :, n:2 * n] = g


def _rwkv_features(u_r, v_first_src, p, n_lat_tiles):
    b, l, c = u_r.shape
    tm = ROW_TILE
    n_tiles = l // tm
    n = RWKV_DIM
    hb = tm // HALO
    has_vres = v_first_src is not None
    row = lambda bb, i: (bb, i, 0)
    in_specs = [pl.BlockSpec((1, tm, c), row),
                pl.BlockSpec((1, HALO, c), lambda bb, i: (bb, jnp.maximum(i * hb - 1, 0), 0)),
                pl.BlockSpec((1, HALO, c), lambda bb, i: (bb, jnp.minimum((i + 1) * hb, l // HALO - 1), 0))]
    args = [u_r, u_r, u_r]
    if has_vres:
        in_specs.append(pl.BlockSpec((1, tm, n), lambda bb, i: (bb, i, 1)))
        args.append(v_first_src)
    consts = [p["mu"], p["vec"], p["w0"], p["a0"], p["w2"], p["a2"], p["g2"], p["seg"]]
    if has_vres:
        consts += [p["v0"], p["v1"], p["v2"]]
    in_specs += [_const_spec(x.shape) for x in consts]
    args += consts
    widths = (3 * n, 3 * n, 3 * n, 2 * n)
    return pl.pallas_call(
        functools.partial(_feat_kernel, has_vres=has_vres, n_lat_tiles=n_lat_tiles, n_tiles=n_tiles),
        grid=(b, n_tiles),
        in_specs=in_specs,
        out_specs=[pl.BlockSpec((1, tm, w), row) for w in widths],
        out_shape=[jax.ShapeDtypeStruct((b, l, w), F32) for w in widths],
        scratch_shapes=[pltpu.VMEM((tm + 2 * HALO, c), F32)],
        compiler_params=_cparams(2),
        name="rwkv_features",
    )(*args)


def _wkv_direction(fc_ref, fd_ref, y_ref, s_ref, reverse):
    c = WKV_CHUNK
    n = RWKV_DIM
    hd = RWKV_HEAD
    r = fc_ref[0, :, 0:n]
    v = fc_ref[0, :, n:2 * n]
    kk = fc_ref[0, :, 2 * n:3 * n]
    logw = fd_ref[0, :, 0:n]
    key = fd_ref[0, :, n:2 * n]
    rem = fd_ref[0, :, 2 * n:3 * n]

    ti = lax.broadcasted_iota(jnp.int32, (c, c), 0)
    si = lax.broadcasted_iota(jnp.int32, (c, c), 1)
    if reverse:
        incl, strict = si >= ti, si > ti
    else:
        incl, strict = si <= ti, si < ti
    cum = _dot_exact_lhs(jnp.where(incl, 1.0, 0.0).astype(BF16), logw)
    total = cum[0:1, :] if reverse else cum[c - 1:c, :]
    c0 = 0.5 * total
    e_neg = jnp.exp(c0 - cum)
    a_t = -kk * jnp.exp(cum - logw - c0)
    r_t = r * jnp.exp(cum - c0)
    b_t = rem * e_neg
    k_t = key * e_neg
    e_half = jnp.exp(c0)
    e_tot = jnp.exp(total)
    eye = jnp.where(ti == si, 1.0, 0.0)

    level_masks = []
    s = 1
    while s < c:
        sh = int(math.log2(s))
        bt = lax.shift_right_logical(ti, sh)
        bs = lax.shift_right_logical(si, sh)
        if reverse:
            level_masks.append(jnp.logical_and((bt & 1) == 0, bs == bt + 1))
        else:
            level_masks.append(jnp.logical_and((bt & 1) == 1, bs == bt - 1))
        s *= 2

    for h in range(RWKV_HEADS):
        sl = slice(h * hd, (h + 1) * hd)
        lhs = jnp.concatenate([a_t[:, sl], r_t[:, sl]], axis=0).astype(BF16)
        rhs = jnp.concatenate([b_t[:, sl], k_t[:, sl]], axis=0).astype(BF16)
        v_h = v[:, sl]
        s0 = s_ref[h]
        g = _dot_nt(lhs, rhs)
        a_s = _dot_nt(lhs, s0 * e_half[:, sl])
        a_ab = jnp.where(strict, g[:c, :c], 0.0)
        a_ak = jnp.where(strict, g[:c, c:], 0.0)
        a_rb = jnp.where(incl, g[c:, :c], 0.0)
        a_rk = jnp.where(incl, g[c:, c:], 0.0)
        t_inv = eye + jnp.where(level_masks[0], a_ab, 0.0)
        for m in level_masks[1:]:
            tb = t_inv.astype(BF16)
            t_inv = t_inv + _dot(tb, _dot(jnp.where(m, a_ab, 0.0), tb))
        u = _dot(t_inv, a_s[:c] + _dot(a_ak, v_h))
        uv = jnp.concatenate([u, v_h], axis=0).astype(BF16)
        y_ref[0, :, sl] = a_s[c:] + _dot(jnp.concatenate([a_rb, a_rk], axis=1), uv)
        upd = lax.dot_general(uv, rhs, (((0,), (0,)), ((), ())), preferred_element_type=F32)
        s_ref[h] = s0 * e_tot[:, sl] + upd * e_half[:, sl]


def _wkv_kernel(fcf_ref, ff_ref, fcb_ref, fb_ref, yf_ref, yb_ref, sf_ref, sb_ref):
    @pl.when(pl.program_id(1) == 0)
    def _():
        sf_ref[...] = jnp.zeros_like(sf_ref)
        sb_ref[...] = jnp.zeros_like(sb_ref)

    _wkv_direction(fcf_ref, ff_ref, yf_ref, sf_ref, False)
    _wkv_direction(fcb_ref, fb_ref, yb_ref, sb_ref, True)


def _wkv_scan(fc, ff, fb, n_lat):
    b, l, w = fc.shape
    c = WKV_CHUNK
    nl = n_lat // c
    nc = l // c - nl
    fwd = lambda bb, j: (bb, jnp.where(j < nc, nl + j, j - nc), 0)
    bwd = lambda bb, j: (bb, nl + nc - 1 - j, 0)
    return pl.pallas_call(
        _wkv_kernel,
        grid=(b, nl + nc),
        in_specs=[pl.BlockSpec((1, c, w), fwd), pl.BlockSpec((1, c, w), fwd),
                  pl.BlockSpec((1, c, w), bwd), pl.BlockSpec((1, c, w), bwd)],
        out_specs=[pl.BlockSpec((1, c, RWKV_DIM), fwd), pl.BlockSpec((1, c, RWKV_DIM), bwd)],
        out_shape=[jax.ShapeDtypeStruct((b, l, RWKV_DIM), F32)] * 2,
        scratch_shapes=[pltpu.VMEM((RWKV_HEADS, RWKV_HEAD, RWKV_HEAD), F32)] * 2,
        compiler_params=_cparams(2),
        name="wkv_scan",
    )(fc, ff, fc, fb)


def _attn_kernel(sink_ref, q_ref, kv_ref, o_ref, *, n_lat):
    j = pl.program_id(1)
    qb = ATT_BLOCK
    hd = ATT_HEAD
    n_ctx = kv_ref.shape[1] - n_lat
    n_win = 3 * qb
    is_lat = j * qb < n_lat
    ws = pl.multiple_of(jnp.clip((j - 1) * qb, 0, n_lat - n_win), qb)
    q_pos = j * qb + lax.broadcasted_iota(jnp.int32, (qb, n_win), 0)
    k_pos = ws + lax.broadcasted_iota(jnp.int32, (qb, n_win), 1)
    valid = jnp.logical_and(jnp.abs(q_pos - k_pos) <= WINDOW, is_lat)
    q = q_ref[0]
    kv_c = kv_ref[0, n_lat:n_lat + n_ctx, :]
    kv_w = kv_ref[0, pl.ds(ws, n_win), :]
    for g in range(ATT_KV_HEADS):
        k_c = kv_c[:, g * hd:(g + 1) * hd]
        k_w = kv_w[:, g * hd:(g + 1) * hd]
        v_c = kv_c[:, ATT_KV_DIM + g * hd:ATT_KV_DIM + (g + 1) * hd]
        v_w = kv_w[:, ATT_KV_DIM + g * hd:ATT_KV_DIM + (g + 1) * hd]
        for hh in range(ATT_GROUP):
            h = g * ATT_GROUP + hh
            q_h = q[:, h * hd:(h + 1) * hd]
            s_c = _dot_nt(q_h, k_c)
            s_w = jnp.where(valid, _dot_nt(q_h, k_w), NEG_INF)
            sink = sink_ref[h]
            m = jnp.maximum(jnp.maximum(jnp.max(s_c, axis=-1, keepdims=True),
                                        jnp.max(s_w, axis=-1, keepdims=True)), sink)
            p_c = jnp.exp(s_c - m)
            p_w = jnp.exp(s_w - m)
            den = (jnp.sum(p_c, axis=-1, keepdims=True) + jnp.sum(p_w, axis=-1, keepdims=True)
                   + jnp.exp(sink - m))
            o = (_dot(p_c, v_c) + _dot(p_w, v_w)) / den
            o_ref[0, :, h * hd:(h + 1) * hd] = o.astype(o_ref.dtype)


def _attention(q, kv, sink, n_lat, n_rows):
    b, l, _ = q.shape
    qb = ATT_BLOCK
    return pl.pallas_call(
        functools.partial(_attn_kernel, n_lat=n_lat),
        grid=(b, n_rows // qb),
        in_specs=[pl.BlockSpec(memory_space=pltpu.SMEM),
                  pl.BlockSpec((1, qb, ATT_Q_DIM), lambda bb, j: (bb, j, 0)),
                  pl.BlockSpec((1, l, kv.shape[2]), lambda bb, j: (bb, 0, 0))],
        out_specs=pl.BlockSpec((1, qb, ATT_Q_DIM), lambda bb, j: (bb, j, 0)),
        out_shape=jax.ShapeDtypeStruct((b, n_rows, ATT_Q_DIM), BF16),
        compiler_params=_cparams(2),
        name="windowed_attention",
    )(sink, q, kv)


def _dft_kernel(z_ref, ct_ref, st_ref, o_ref):
    n = FOURIER_DIM
    o_ref[0] = (jnp.dot(ct_ref[...], z_ref[0, :, 0:n], preferred_element_type=F32)
                - jnp.dot(st_ref[...], z_ref[0, :, n:2 * n], preferred_element_type=F32)).astype(o_ref.dtype)


def _token_dft(z, ct, st, seg_rows, seg_block):
    b = z.shape[0]
    tq = ROW_TILE
    return pl.pallas_call(
        _dft_kernel,
        grid=(b, seg_rows // tq),
        in_specs=[pl.BlockSpec((1, seg_rows, z.shape[2]), lambda bb, i: (bb, seg_block, 0)),
                  pl.BlockSpec((tq, seg_rows), lambda bb, i: (i, 0)),
                  pl.BlockSpec((tq, seg_rows), lambda bb, i: (i, 0))],
        out_specs=pl.BlockSpec((1, tq, FOURIER_DIM), lambda bb, i: (bb, i, 0)),
        out_shape=jax.ShapeDtypeStruct((b, seg_rows, FOURIER_DIM), BF16),
        compiler_params=_cparams(2),
        name="token_dft",
    )(z, ct, st)


def _merge_kernel(h_ref, mod_ref, yf_ref, yb_ref, ro_ref, ya_ref, yd_ref, ug_ref, ln_ref, avg_ref,
                  wbr_ref, wba_ref, wbf_ref, wo_ref, gp_ref, o_ref):
    n = RWKV_DIM
    d = h_ref.shape[2]
    avg = avg_ref[...]
    y = yf_ref[0] + yb_ref[0]
    dev = y - _dot_exact_rhs(y, avg)
    var = _dot_exact_rhs(dev * dev, avg)
    yn = dev * lax.rsqrt(var + LNX_EPS) * ln_ref[0:1, :] + ln_ref[1:2, :]
    y_r = (yn + ro_ref[0, :, 0:n]) * ro_ref[0, :, n:2 * n]
    ug = ug_ref[0]
    mix = _sigmoid(ug[:, 0:d].astype(F32)) * _dot(y_r, wbr_ref[...])
    mix += _sigmoid(ug[:, d:2 * d].astype(F32)) * jnp.dot(ya_ref[0], wba_ref[...], preferred_element_type=F32)
    mix += _sigmoid(ug[:, 2 * d:3 * d].astype(F32)) * jnp.dot(yd_ref[0], wbf_ref[...], preferred_element_type=F32)
    o = _dot(mix, wo_ref[...])
    o = o * lax.rsqrt(jnp.mean(o * o, axis=-1, keepdims=True) + EPS) * gp_ref[...]
    o_ref[0] = h_ref[0] + mod_ref[0, 0, 2:3, :] * o


def _merge(h, modtab, yf, yb, ro, ya, yd, ug, p, n_rows, n_lat_tiles):
    b, _, d = h.shape
    tm = ROW_TILE
    row = lambda bb, i: (bb, i, 0)
    consts = [p["ln"], p["avg"], p["wbr"], p["wba"], p["wbf"], p["wo"], p["gpost"]]
    return pl.pallas_call(
        _merge_kernel,
        grid=(b, n_rows // tm),
        in_specs=[pl.BlockSpec((1, tm, d), row),
                  pl.BlockSpec((1, 1, 6, d), lambda bb, i: (bb, jnp.where(i < n_lat_tiles, 0, 1), 0, 0)),
                  pl.BlockSpec((1, tm, RWKV_DIM), row), pl.BlockSpec((1, tm, RWKV_DIM), row),
                  pl.BlockSpec((1, tm, 2 * RWKV_DIM), row),
                  pl.BlockSpec((1, tm, ATT_Q_DIM), row), pl.BlockSpec((1, tm, FOURIER_DIM), row),
                  pl.BlockSpec((1, tm, 3 * d), row)] + [_const_spec(x.shape) for x in consts],
        out_specs=pl.BlockSpec((1, tm, d), row),
        out_shape=jax.ShapeDtypeStruct((b, n_rows, d), F32),
        compiler_params=_cparams(2),
        name="branch_merge",
    )(h, modtab, yf, yb, ro, ya, yd, ug, *consts)


def _ffn_kernel(h_ref, hp_ref, hn_ref, mod_ref, gpre_ref, upg_ref, upv_ref, cw_ref, dn_ref, gpost_ref, o_ref,
                zbuf_ref, acc_ref, *, n_lat_tiles, n_tiles):
    i = pl.program_id(1)
    tm = h_ref.shape[1]
    prev_ok = jnp.logical_and(i != 0, i != n_lat_tiles)
    next_ok = jnp.logical_and(i != n_lat_tiles - 1, i != n_tiles - 1)
    h = h_ref[0]
    x = jnp.concatenate([hp_ref[0], h, hn_ref[0]], axis=0)
    f = _rms_mod(x, gpre_ref[...], mod_ref[0, 0, 3:4, :], mod_ref[0, 0, 4:5, :]).astype(BF16)
    rid = lax.broadcasted_iota(jnp.int32, (tm + 2 * HALO, 1), 0)
    dead = jnp.logical_or(jnp.logical_and(rid < HALO, jnp.logical_not(prev_ok)),
                          jnp.logical_and(rid >= HALO + tm, jnp.logical_not(next_ok)))
    acc_ref[...] = jnp.zeros_like(acc_ref)

    def body(c, carry):
        zbuf_ref[...] = jnp.where(dead, 0.0, jnp.dot(f, upg_ref[c], preferred_element_type=F32))
        cw = cw_ref[c]
        zg = (cw[0:1, :] * zbuf_ref[HALO - 1:HALO - 1 + tm, :] + cw[1:2, :] * zbuf_ref[HALO:HALO + tm, :]
              + cw[2:3, :] * zbuf_ref[HALO + 1:HALO + 1 + tm, :] + cw[3:4, :])
        zv = jnp.dot(f[HALO:HALO + tm], upv_ref[c], preferred_element_type=F32)
        act = 0.5 * zg * (1.0 + jnp.tanh(0.7978845608028654 * (zg + 0.044715 * zg * zg * zg)))
        acc_ref[...] += _dot(act * zv, dn_ref[c])
        return carry

    lax.fori_loop(0, upg_ref.shape[0], body, 0)
    o = acc_ref[...]
    o = o * lax.rsqrt(jnp.mean(o * o, axis=-1, keepdims=True) + EPS) * gpost_ref[...]
    o_ref[0] = h + mod_ref[0, 0, 5:6, :] * o


def _ffn(h, modtab, p, n_rows, n_lat_tiles):
    b, l, d = h.shape
    tm = ROW_TILE
    hb = tm // HALO
    n_tiles = l // tm
    row = lambda bb, i: (bb, i, 0)
    consts = [p["gpre"], p["upg"], p["upv"], p["cw"], p["dn"], p["gpost"]]
    return pl.pallas_call(
        functools.partial(_ffn_kernel, n_lat_tiles=n_lat_tiles, n_tiles=n_tiles),
        grid=(b, n_rows // tm),
        in_specs=[pl.BlockSpec((1, tm, d), row),
                  pl.BlockSpec((1, HALO, d), lambda bb, i: (bb, jnp.maximum(i * hb - 1, 0), 0)),
                  pl.BlockSpec((1, HALO, d), lambda bb, i: (bb, jnp.minimum((i + 1) * hb, l // HALO - 1), 0)),
                  pl.BlockSpec((1, 1, 6, d), lambda bb, i: (bb, jnp.where(i < n_lat_tiles, 0, 1), 0, 0))]
        + [_const_spec(x.shape) for x in consts],
        out_specs=pl.BlockSpec((1, tm, d), row),
        out_shape=jax.ShapeDtypeStruct((b, n_rows, d), F32),
        scratch_shapes=[pltpu.VMEM((tm + 2 * HALO, FF_CHUNK), F32), pltpu.VMEM((tm, d), F32)],
        compiler_params=_cparams(2),
        name="conv_ffn",
    )(h, h, h, modtab, *consts)


def _rope_tables(n_lat, n_ctx):
    t = jnp.arange(n_lat)
    row_id = (t // GRID_W).astype(F32)
    col_id = (t % GRID_W).astype(F32)
    inv = ROPE_BASE ** (-jnp.arange(ROPE_FREQS, dtype=F32) / ROPE_FREQS)
    d = np.arange(ATT_HEAD)
    freq = d % ROPE_FREQS
    ang = jnp.where((d // (2 * ROPE_FREQS) == 0)[None, :], row_id[:, None], col_id[:, None]) * inv[freq][None, :]
    sign = np.where((d // ROPE_FREQS) % 2 == 0, -1.0, 1.0).astype(np.float32)
    cos = jnp.concatenate([jnp.cos(ang), jnp.ones((n_ctx, ATT_HEAD), F32)], axis=0)
    sin = jnp.concatenate([jnp.sin(ang) * sign[None, :], jnp.zeros((n_ctx, ATT_HEAD), F32)], axis=0)
    return jnp.tile(cos, (1, ATT_HEADS)), jnp.tile(sin, (1, ATT_HEADS))


def _dft_mats(n):
    idx = jnp.arange(n, dtype=jnp.int32)
    ang = ((idx[:, None] * idx[None, :]) % n).astype(F32) * (2.0 * math.pi / n)
    scale = 1.0 / math.sqrt(n)
    return jnp.cos(ang) * scale, jnp.sin(ang) * scale


def _block_diag(m, groups):
    return jnp.kron(jnp.eye(groups, dtype=m.dtype), m)


def kernel(x, c, ctx, c_ctx, mod_w, mod_b, norm_mix_pre, norm_mix_post, norm_ffn_pre, norm_ffn_post, w_in, rwkv_mu, rwkv_w0, rwkv_w2, rwkv_a0, rwkv_a2, rwkv_g2, rwkv_k_k, rwkv_k_a, rwkv_r_k, rwkv_lnx_w, rwkv_lnx_b, rwkv_v0, rwkv_v1, rwkv_v2, attn_sink, w_branch_rwkv, w_branch_attn, w_branch_fourier, w_out, ffn_up, ffn_conv_w, ffn_conv_b, ffn_down):
    b, n_lat, d = x.shape
    n_ctx = ctx.shape[1]
    depth = mod_w.shape[0]
    d_ff = ffn_down.shape[1]
    l = n_lat + n_ctx
    tm = ROW_TILE
    assert n_lat % tm == 0 and n_ctx % tm == 0 and n_lat % n_ctx == 0
    assert n_lat >= 3 * ATT_BLOCK and d_ff % FF_CHUNK == 0
    n_lat_tiles = n_lat // tm

    cos_t, sin_t = _rope_tables(n_lat, n_ctx)
    cg, sg = _dft_mats(FOURIER_GROUP_DIM)
    dft_c = jnp.concatenate([_block_diag(cg, FOURIER_GROUPS), _block_diag(sg, FOURIER_GROUPS)], axis=1).astype(BF16)
    ct_lat, st_lat = (m.astype(BF16) for m in _dft_mats(n_lat))
    ct_ctx, st_ctx = (m.astype(BF16) for m in _dft_mats(n_ctx))
    seg = _block_diag(jnp.ones((RWKV_HEAD, RWKV_HEAD), F32), RWKV_HEADS).astype(BF16)
    avg = (seg.astype(F32) / RWKV_HEAD).astype(BF16)

    pad = (-(b + 1)) % 8
    cvec = jnp.concatenate([c, c_ctx[None, :], jnp.zeros((pad, d), F32)], axis=0)
    mod = _modulation(cvec, mod_w, mod_b)

    h = jnp.concatenate([x, ctx], axis=1)
    v_first = None
    for layer in range(depth):
        last = layer == depth - 1
        lat = mod[layer, :b].reshape(b, 1, 6, d)
        cm = jnp.broadcast_to(mod[layer, b].reshape(1, 1, 6, d), (b, 1, 6, d))
        modtab = jnp.concatenate([lat, cm], axis=1)

        u_r, q, kv, z, u_g = _inproj(h, modtab, norm_mix_pre[layer], w_in[layer], dft_c, cos_t, sin_t, n_lat_tiles)

        fp = {
            "mu": rwkv_mu[layer],
            "vec": jnp.stack([rwkv_k_k[layer], rwkv_k_a[layer], rwkv_r_k[layer]]),
            "w0": rwkv_w0[layer], "a0": rwkv_a0[layer],
            "w2": rwkv_w2[layer].astype(BF16), "a2": rwkv_a2[layer].astype(BF16),
            "g2": rwkv_g2[layer].astype(BF16), "seg": seg,
        }
        if layer > 0:
            lp = 128 - MV_LORA
            fp["v0"] = rwkv_v0[layer - 1].reshape(1, RWKV_DIM)
            fp["v1"] = jnp.pad(rwkv_v1[layer - 1], ((0, 0), (0, lp))).astype(BF16)
            fp["v2"] = jnp.pad(rwkv_v2[layer - 1], ((0, lp), (0, 0))).astype(BF16)
        fc, ff, fb, ro = _rwkv_features(u_r, v_first, fp, n_lat_tiles)
        if layer == 0:
            v_first = fc
        y_fwd, y_bwd = _wkv_scan(fc, ff, fb, n_lat)

        n_rows = n_lat if last else l
        y_att = _attention(q, kv, attn_sink[layer], n_lat, n_rows)
        y_dft = _token_dft(z, ct_lat, st_lat, n_lat, 0)
        if not last:
            y_dft = jnp.concatenate([y_dft, _token_dft(z, ct_ctx, st_ctx, n_ctx, n_lat // n_ctx)], axis=1)

        mp = {
            "ln": jnp.stack([rwkv_lnx_w[layer], rwkv_lnx_b[layer]]), "avg": avg,
            "wbr": w_branch_rwkv[layer].astype(BF16), "wba": w_branch_attn[layer].astype(BF16),
            "wbf": w_branch_fourier[layer].astype(BF16), "wo": w_out[layer].astype(BF16),
            "gpost": norm_mix_post[layer].reshape(1, d),
        }
        h = _merge(h, modtab, y_fwd, y_bwd, ro, y_att, y_dft, u_g, mp, n_rows, n_lat_tiles)

        nch = d_ff // FF_CHUNK
        up = ffn_up[layer].astype(BF16)
        pp = {
            "gpre": norm_ffn_pre[layer].reshape(1, d),
            "upg": up[:, :d_ff].reshape(d, nch, FF_CHUNK).transpose(1, 0, 2),
            "upv": up[:, d_ff:].reshape(d, nch, FF_CHUNK).transpose(1, 0, 2),
            "cw": jnp.concatenate([ffn_conv_w[layer], ffn_conv_b[layer][None, :]], axis=0)
                  .reshape(4, nch, FF_CHUNK).transpose(1, 0, 2),
            "dn": ffn_down[layer].astype(BF16).reshape(nch, FF_CHUNK, d),
            "gpost": norm_ffn_post[layer].reshape(1, d),
        }
        h = _ffn(h, modtab, pp, n_rows, n_lat_tiles)
    return h
```

```python
import functools
import math

import numpy as np
import jax
import jax.numpy as jnp
from jax import lax
from jax.experimental import pallas as pl
from jax.experimental.pallas import tpu as pltpu

F32 = jnp.float32
BF16 = jnp.bfloat16

GRID_W = 64
RWKV_HEADS = 4
RWKV_HEAD = 64
RWKV_DIM = RWKV_HEADS * RWKV_HEAD
DECAY_LORA = 64
AAA_LORA = 64
MV_LORA = 32
GATE_LORA = 128
LNX_EPS = 64e-5
ATT_HEADS = 8
ATT_KV_HEADS = 2
ATT_GROUP = ATT_HEADS // ATT_KV_HEADS
ATT_HEAD = 64
ATT_Q_DIM = ATT_HEADS * ATT_HEAD
ATT_KV_DIM = ATT_KV_HEADS * ATT_HEAD
ATT_SCALE = ATT_HEAD ** -0.5
WINDOW = 128
ROPE_BASE = 10000.0
ROPE_FREQS = ATT_HEAD // 4
NEG_INF = -1e30
FOURIER_GROUPS = 4
FOURIER_GROUP_DIM = 64
FOURIER_DIM = FOURIER_GROUPS * FOURIER_GROUP_DIM
EPS = 1e-6
RWKV_COLS = 3 * RWKV_DIM + 2 * DECAY_LORA + 2 * AAA_LORA + GATE_LORA

ROW_TILE = 256
HALO = 8
WKV_CHUNK = 64
WKV_BATCH = 4
ATT_BLOCK = 128
FF_CHUNK = 256
VMEM_LIMIT = 56 * 1024 * 1024


def _cparams(n_axes):
    return pltpu.CompilerParams(dimension_semantics=("arbitrary",) * n_axes,
                                vmem_limit_bytes=VMEM_LIMIT)


def _dot(a, b):
    return jnp.dot(a.astype(BF16), b.astype(BF16), preferred_element_type=F32)


def _dot_nt(a, b):
    return lax.dot_general(a.astype(BF16), b.astype(BF16), (((1,), (1,)), ((), ())),
                           preferred_element_type=F32)


def _split3(x):
    hi = x.astype(BF16)
    r1 = x - hi.astype(F32)
    mid = r1.astype(BF16)
    lo = (r1 - mid.astype(F32)).astype(BF16)
    return hi, mid, lo


def _dot_exact_rhs(x, m):
    hi, mid, lo = _split3(x)
    return (jnp.dot(hi, m, preferred_element_type=F32) + jnp.dot(mid, m, preferred_element_type=F32)
            + jnp.dot(lo, m, preferred_element_type=F32))


def _dot_exact_lhs(m, x):
    hi, mid, lo = _split3(x)
    return (jnp.dot(m, hi, preferred_element_type=F32) + jnp.dot(m, mid, preferred_element_type=F32)
            + jnp.dot(m, lo, preferred_element_type=F32))


def _sigmoid(x):
    return 1.0 / (1.0 + jnp.exp(-x))


def _const_spec(shape):
    nd = len(shape)
    return pl.BlockSpec(shape, lambda *_: (0,) * nd, pipeline_mode=pl.Buffered(1))


def _rms_mod(x, g, shift, scale):
    y = x * lax.rsqrt(jnp.mean(x * x, axis=-1, keepdims=True) + EPS) * g
    return y * (1.0 + scale) + shift


def _mod_kernel(c_ref, w_ref, b_ref, o_ref):
    x = c_ref[...]
    o_ref[0] = _dot(x * _sigmoid(x), w_ref[0]) + b_ref[0]


def _modulation(cvec, mod_w, mod_b):
    depth, d, n = mod_w.shape
    rows = cvec.shape[0]
    tn = 1536
    return pl.pallas_call(
        _mod_kernel,
        grid=(depth, n // tn),
        in_specs=[pl.BlockSpec((rows, d), lambda l, j: (0, 0)),
                  pl.BlockSpec((1, d, tn), lambda l, j: (l, 0, j)),
                  pl.BlockSpec((1, 1, tn), lambda l, j: (l, 0, j))],
        out_specs=pl.BlockSpec((1, rows, tn), lambda l, j: (l, 0, j)),
        out_shape=jax.ShapeDtypeStruct((depth, rows, n), F32),
        compiler_params=_cparams(2),
        name="adaln_modulation",
    )(cvec, mod_w.astype(BF16), mod_b.reshape(depth, 1, n))


def _rope(x, cos, sin_signed):
    n = x.shape[1]
    lane = lax.broadcasted_iota(jnp.int32, x.shape, 1)
    first = (lane & ROPE_FREQS) == 0
    partner = jnp.where(first, pltpu.roll(x, n - ROPE_FREQS, 1), pltpu.roll(x, ROPE_FREQS, 1))
    return x * cos + partner * sin_signed


def _inproj_kernel(h_ref, mod_ref, g_ref, wr_ref, wq_ref, wkv_ref, wf_ref, wg_ref, dft_ref, cos_ref, sin_ref,
                   ur_ref, q_ref, kv_ref, z_ref, ug_ref):
    a = _rms_mod(h_ref[0], g_ref[...], mod_ref[0, 0, 0:1, :], mod_ref[0, 0, 1:2, :]).astype(BF16)
    ur_ref[0] = jnp.dot(a, wr_ref[...], preferred_element_type=F32)
    cos = cos_ref[...]
    sin = sin_ref[...]
    q = jnp.dot(a, wq_ref[...], preferred_element_type=F32)
    q_ref[0] = (_rope(q, cos, sin) * ATT_SCALE).astype(BF16)
    kv = jnp.dot(a, wkv_ref[...], preferred_element_type=F32)
    k = _rope(kv[:, :ATT_KV_DIM], cos[:, :ATT_KV_DIM], sin[:, :ATT_KV_DIM])
    kv_ref[0, :, :ATT_KV_DIM] = k.astype(BF16)
    kv_ref[0, :, ATT_KV_DIM:] = kv[:, ATT_KV_DIM:].astype(BF16)
    uf = jnp.dot(a, wf_ref[...], preferred_element_type=F32)
    z_ref[0] = _dot(uf, dft_ref[...]).astype(BF16)
    ug_ref[0] = jnp.dot(a, wg_ref[...], preferred_element_type=F32).astype(BF16)


def _inproj(h, modtab, g, w_in, dft_c, cos_t, sin_t, n_lat_tiles):
    b, l, d = h.shape
    tm = ROW_TILE
    n_g = w_in.shape[1] - (RWKV_COLS + ATT_Q_DIM + 2 * ATT_KV_DIM + FOURIER_DIM)
    o = np.cumsum([0, RWKV_COLS, ATT_Q_DIM, 2 * ATT_KV_DIM, FOURIER_DIM, n_g])
    wb = w_in.astype(BF16)
    ws = [wb[:, o[i]:o[i + 1]] for i in range(5)]
    row = lambda bb, i: (bb, i, 0)
    outs = [(RWKV_COLS, F32), (ATT_Q_DIM, BF16), (2 * ATT_KV_DIM, BF16), (2 * FOURIER_DIM, BF16), (n_g, BF16)]
    return pl.pallas_call(
        _inproj_kernel,
        grid=(b, l // tm),
        in_specs=[pl.BlockSpec((1, tm, d), row),
                  pl.BlockSpec((1, 1, 6, d), lambda bb, i: (bb, jnp.where(i < n_lat_tiles, 0, 1), 0, 0)),
                  _const_spec((1, d))] + [_const_spec(w.shape) for w in ws] + [
                  _const_spec(dft_c.shape),
                  pl.BlockSpec((tm, ATT_Q_DIM), lambda bb, i: (i, 0)),
                  pl.BlockSpec((tm, ATT_Q_DIM), lambda bb, i: (i, 0))],
        out_specs=[pl.BlockSpec((1, tm, n), row) for n, _ in outs],
        out_shape=[jax.ShapeDtypeStruct((b, l, n), dt) for n, dt in outs],
        compiler_params=_cparams(2),
        name="in_projection",
    )(h, modtab, g.reshape(1, d), *ws, dft_c, cos_t, sin_t)


def _feat_kernel(*refs, has_vres, n_lat_tiles, n_tiles):
    if has_vres:
        (u_ref, up_ref, un_ref, vf_ref, mu_ref, vec_ref, w0_ref, a0_ref, w2_ref, a2_ref, g2_ref, seg_ref,
         v0_ref, v1_ref, v2_ref, fc_ref, ff_ref, fb_ref, ro_ref, buf_ref) = refs
    else:
        (u_ref, up_ref, un_ref, mu_ref, vec_ref, w0_ref, a0_ref, w2_ref, a2_ref, g2_ref, seg_ref,
         fc_ref, ff_ref, fb_ref, ro_ref, buf_ref) = refs
    i = pl.program_id(1)
    tm = u_ref.shape[1]
    prev_ok = jnp.logical_and(i != 0, i != n_lat_tiles)
    next_ok = jnp.logical_and(i != n_lat_tiles - 1, i != n_tiles - 1)
    u = u_ref[0]
    buf_ref[0:HALO, :] = jnp.where(prev_ok, up_ref[0], 0.0)
    buf_ref[HALO:HALO + tm, :] = u
    buf_ref[HALO + tm:, :] = jnp.where(next_ok, un_ref[0], 0.0)
    u_prev = buf_ref[HALO - 1:HALO - 1 + tm, :]
    u_next = buf_ref[HALO + 1:HALO + 1 + tm, :]
    us = u + mu_ref[0:1, :] * (u_prev - u) + mu_ref[1:2, :] * (u_next - u)

    n = RWKV_DIM
    r = us[:, 0:n]
    k = us[:, n:2 * n]
    v = us[:, 2 * n:3 * n]
    o = 3 * n
    xw = (us[:, o:o + DECAY_LORA], us[:, o + DECAY_LORA:o + 2 * DECAY_LORA])
    o += 2 * DECAY_LORA
    xa = (us[:, o:o + AAA_LORA], us[:, o + AAA_LORA:o + 2 * AAA_LORA])
    o += 2 * AAA_LORA
    xg = us[:, o:o + GATE_LORA]
    k_k = vec_ref[0:1, :]
    k_a = vec_ref[1:2, :]
    r_k = vec_ref[2:3, :]
    seg = seg_ref[...]

    if has_vres:
        gate = _sigmoid(v0_ref[...] + _dot(_dot(v, v1_ref[...]), v2_ref[...]))
        v = v + (vf_ref[0] - v) * gate
    kk = k * k_k
    kk = kk * lax.rsqrt(jnp.maximum(_dot_exact_rhs(kk * kk, seg), 1e-24))
    keys = []
    for d, out_ref in enumerate((ff_ref, fb_ref)):
        z = w0_ref[d:d + 1, :] + _dot(jnp.tanh(xw[d]), w2_ref[d])
        softplus = jnp.maximum(-z, 0.0) + jnp.log(1.0 + jnp.exp(-jnp.abs(z)))
        out_ref[0, :, 0:n] = -jnp.exp(-softplus - 0.5)
        a = _sigmoid(a0_ref[d:d + 1, :] + _dot(xa[d], a2_ref[d]))
        key = k * (1.0 + (a - 1.0) * k_a)
        keys.append(key)
        out_ref[0, :, n:2 * n] = key
        out_ref[0, :, 2 * n:3 * n] = kk * a
    g = _dot(_sigmoid(xg), g2_ref[...])
    bonus = _dot_exact_rhs(r * (0.5 * (keys[0] + keys[1])) * r_k, seg) * v
    fc_ref[0, :, 0:n] = r
    fc_ref[0, :, n:2 * n] = v
    fc_ref[0, :, 2 * n:3 * n] = kk
    ro_ref[0, :, 0:n] = bonus
    ro_ref[0, :, n:2 * n] = g


def _rwkv_features(u_r, v_first_src, p, n_lat_tiles):
    b, l, c = u_r.shape
    tm = ROW_TILE
    n_tiles = l // tm
    n = RWKV_DIM
    hb = tm // HALO
    has_vres = v_first_src is not None
    row = lambda bb, i: (bb, i, 0)
    in_specs = [pl.BlockSpec((1, tm, c), row),
                pl.BlockSpec((1, HALO, c), lambda bb, i: (bb, jnp.maximum(i * hb - 1, 0), 0)),
                pl.BlockSpec((1, HALO, c), lambda bb, i: (bb, jnp.minimum((i + 1) * hb, l // HALO - 1), 0))]
    args = [u_r, u_r, u_r]
    if has_vres:
        in_specs.append(pl.BlockSpec((1, tm, n), lambda bb, i: (bb, i, 1)))
        args.append(v_first_src)
    consts = [p["mu"], p["vec"], p["w0"], p["a0"], p["w2"], p["a2"], p["g2"], p["seg"]]
    if has_vres:
        consts += [p["v0"], p["v1"], p["v2"]]
    in_specs += [_const_spec(x.shape) for x in consts]
    args += consts
    widths = (3 * n, 3 * n, 3 * n, 2 * n)
    return pl.pallas_call(
        functools.partial(_feat_kernel, has_vres=has_vres, n_lat_tiles=n_lat_tiles, n_tiles=n_tiles),
        grid=(b, n_tiles),
        in_specs=in_specs,
        out_specs=[pl.BlockSpec((1, tm, w), row) for w in widths],
        out_shape=[jax.ShapeDtypeStruct((b, l, w), F32) for w in widths],
        scratch_shapes=[pltpu.VMEM((tm + 2 * HALO, c), F32)],
        compiler_params=_cparams(2),
        name="rwkv_features",
    )(*args)


def _wkv_masks(c, reverse):
    ti = lax.broadcasted_iota(jnp.int32, (c, c), 0)
    si = lax.broadcasted_iota(jnp.int32, (c, c), 1)
    incl, strict = (si >= ti, si > ti) if reverse else (si <= ti, si < ti)
    levels = []
    for sh in range(int(math.log2(c))):
        bt = lax.shift_right_logical(ti, sh)
        bs = lax.shift_right_logical(si, sh)
        if reverse:
            levels.append(jnp.logical_and((bt & 1) == 0, bs == bt + 1))
        else:
            levels.append(jnp.logical_and((bt & 1) == 1, bs == bt - 1))
    return incl, strict, levels, jnp.where(ti == si, 1.0, 0.0)


def _wkv_kernel(fcf_ref, ff_ref, fcb_ref, fb_ref, yf_ref, yb_ref, sf_ref, sb_ref):
    @pl.when(pl.program_id(1) == 0)
    def _():
        sf_ref[...] = jnp.zeros_like(sf_ref)
        sb_ref[...] = jnp.zeros_like(sb_ref)

    c = WKV_CHUNK
    n = RWKV_DIM
    hd = RWKV_HEAD
    nb = fcf_ref.shape[0]
    probs = []
    for fc_ref, fd_ref, y_ref, s_ref, reverse in ((fcf_ref, ff_ref, yf_ref, sf_ref, False),
                                                  (fcb_ref, fb_ref, yb_ref, sb_ref, True)):
        incl, strict, levels, eye = _wkv_masks(c, reverse)
        tri = jnp.where(incl, 1.0, 0.0).astype(BF16)
        cums = [_dot_exact_lhs(tri, fd_ref[bi, :, 0:n]) for bi in range(nb)]
        for bi in range(nb):
            cum = cums[bi]
            total = cum[0:1, :] if reverse else cum[c - 1:c, :]
            c0 = 0.5 * total
            e_neg = jnp.exp(c0 - cum)
            a_t = -fc_ref[bi, :, 2 * n:3 * n] * jnp.exp(cum - fd_ref[bi, :, 0:n] - c0)
            r_t = fc_ref[bi, :, 0:n] * jnp.exp(cum - c0)
            b_t = fd_ref[bi, :, 2 * n:3 * n] * e_neg
            k_t = fd_ref[bi, :, n:2 * n] * e_neg
            e_half = jnp.exp(c0)
            e_tot = jnp.exp(total)
            for h in range(RWKV_HEADS):
                sl = slice(h * hd, (h + 1) * hd)
                probs.append(dict(
                    lhs=jnp.concatenate([a_t[:, sl], r_t[:, sl]], axis=0).astype(BF16),
                    rhs=jnp.concatenate([b_t[:, sl], k_t[:, sl]], axis=0).astype(BF16),
                    v=fc_ref[bi, :, n + h * hd:n + (h + 1) * hd], s0=s_ref[bi, h],
                    e_half=e_half[:, sl], e_tot=e_tot[:, sl], incl=incl, strict=strict, levels=levels, eye=eye,
                    y_ref=y_ref, s_ref=s_ref, bi=bi, h=h, sl=sl))

    for p in probs:
        p["g"] = _dot_nt(p["lhs"], p["rhs"])
    for p in probs:
        p["a_s"] = _dot_nt(p["lhs"], p["s0"] * p["e_half"])
    for p in probs:
        g = p["g"]
        p["a_ab"] = jnp.where(p["strict"], g[:c, :c], 0.0)
        a_ak = jnp.where(p["strict"], g[:c, c:], 0.0)
        p["a_r"] = jnp.concatenate([jnp.where(p["incl"], g[c:, :c], 0.0),
                                    jnp.where(p["incl"], g[c:, c:], 0.0)], axis=1).astype(BF16)
        p["t"] = p["eye"] + jnp.where(p["levels"][0], p["a_ab"], 0.0)
        p["rhs_u"] = p["a_s"][:c] + _dot(a_ak, p["v"])
    for lvl in range(1, len(probs[0]["levels"])):
        for p in probs:
            p["tb"] = p["t"].astype(BF16)
            p["tmp"] = _dot(jnp.where(p["levels"][lvl], p["a_ab"], 0.0), p["tb"])
        for p in probs:
            p["t"] = p["t"] + _dot(p["tb"], p["tmp"])
    for p in probs:
        p["uv"] = jnp.concatenate([_dot(p["t"], p["rhs_u"]), p["v"]], axis=0).astype(BF16)
    for p in probs:
        p["y_ref"][p["bi"], :, p["sl"]] = p["a_s"][c:] + _dot(p["a_r"], p["uv"])
    for p in probs:
        upd = lax.dot_general(p["uv"], p["rhs"], (((0,), (0,)), ((), ())), preferred_element_type=F32)
        p["s_ref"][p["bi"], p["h"]] = p["s0"] * p["e_tot"] + upd * p["e_half"]


def _wkv_scan(fc, ff, fb, n_lat):
    b, l, w = fc.shape
    c = WKV_CHUNK
    nb = WKV_BATCH
    nl = n_lat // c
    nc = l // c - nl
    fwd = lambda bb, j: (bb, jnp.where(j < nc, nl + j, j - nc), 0)
    bwd = lambda bb, j: (bb, nl + nc - 1 - j, 0)
    return pl.pallas_call(
        _wkv_kernel,
        grid=(b // nb, nl + nc),
        in_specs=[pl.BlockSpec((nb, c, w), fwd), pl.BlockSpec((nb, c, w), fwd),
                  pl.BlockSpec((nb, c, w), bwd), pl.BlockSpec((nb, c, w), bwd)],
        out_specs=[pl.BlockSpec((nb, c, RWKV_DIM), fwd), pl.BlockSpec((nb, c, RWKV_DIM), bwd)],
        out_shape=[jax.ShapeDtypeStruct((b, l, RWKV_DIM), F32)] * 2,
        scratch_shapes=[pltpu.VMEM((nb, RWKV_HEADS, RWKV_HEAD, RWKV_HEAD), F32)] * 2,
        compiler_params=_cparams(2),
        name="wkv_scan",
    )(fc, ff, fc, fb)


def _attn_kernel(sink_ref, q_ref, kv_ref, o_ref, *, n_lat):
    j = pl.program_id(1)
    qb = ATT_BLOCK
    hd = ATT_HEAD
    n_ctx = kv_ref.shape[1] - n_lat
    n_win = 3 * qb
    is_lat = j * qb < n_lat
    ws = pl.multiple_of(jnp.clip((j - 1) * qb, 0, n_lat - n_win), qb)
    col = lax.broadcasted_iota(jnp.int32, (qb, n_ctx + n_win), 1)
    q_pos = j * qb + lax.broadcasted_iota(jnp.int32, (qb, n_ctx + n_win), 0)
    k_pos = ws + col - n_ctx
    valid = jnp.logical_or(col < n_ctx, jnp.logical_and(jnp.abs(q_pos - k_pos) <= WINDOW, is_lat))
    q = q_ref[0]
    kv = jnp.concatenate([kv_ref[0, n_lat:n_lat + n_ctx, :], kv_ref[0, pl.ds(ws, n_win), :]], axis=0)
    heads = range(ATT_HEADS)
    s = [jnp.where(valid, _dot_nt(q[:, h * hd:(h + 1) * hd],
                                  kv[:, (h // ATT_GROUP) * hd:(h // ATT_GROUP + 1) * hd]), NEG_INF) for h in heads]
    m = [jnp.maximum(jnp.max(s[h], axis=-1, keepdims=True), sink_ref[h]) for h in heads]
    p = [jnp.exp(s[h] - m[h]) for h in heads]
    den = [jnp.sum(p[h], axis=-1, keepdims=True) + jnp.exp(sink_ref[h] - m[h]) for h in heads]
    o = [_dot(p[h], kv[:, ATT_KV_DIM + (h // ATT_GROUP) * hd:ATT_KV_DIM + (h // ATT_GROUP + 1) * hd]) for h in heads]
    for h in heads:
        o_ref[0, :, h * hd:(h + 1) * hd] = (o[h] / den[h]).astype(o_ref.dtype)


def _attention(q, kv, sink, n_lat, n_rows):
    b, l, _ = q.shape
    qb = ATT_BLOCK
    return pl.pallas_call(
        functools.partial(_attn_kernel, n_lat=n_lat),
        grid=(b, n_rows // qb),
        in_specs=[pl.BlockSpec(memory_space=pltpu.SMEM),
                  pl.BlockSpec((1, qb, ATT_Q_DIM), lambda bb, j: (bb, j, 0)),
                  pl.BlockSpec((1, l, kv.shape[2]), lambda bb, j: (bb, 0, 0))],
        out_specs=pl.BlockSpec((1, qb, ATT_Q_DIM), lambda bb, j: (bb, j, 0)),
        out_shape=jax.ShapeDtypeStruct((b, n_rows, ATT_Q_DIM), BF16),
        compiler_params=_cparams(2),
        name="windowed_attention",
    )(sink, q, kv)


def _dft_kernel(z_ref, ct_ref, st_ref, o_ref):
    n = FOURIER_DIM
    o_ref[0] = (jnp.dot(ct_ref[...], z_ref[0, :, 0:n], preferred_element_type=F32)
                - jnp.dot(st_ref[...], z_ref[0, :, n:2 * n], preferred_element_type=F32)).astype(o_ref.dtype)


def _token_dft(z, ct, st, seg_rows, seg_block):
    b = z.shape[0]
    tq = ROW_TILE
    return pl.pallas_call(
        _dft_kernel,
        grid=(b, seg_rows // tq),
        in_specs=[pl.BlockSpec((1, seg_rows, z.shape[2]), lambda bb, i: (bb, seg_block, 0)),
                  pl.BlockSpec((tq, seg_rows), lambda bb, i: (i, 0)),
                  pl.BlockSpec((tq, seg_rows), lambda bb, i: (i, 0))],
        out_specs=pl.BlockSpec((1, tq, FOURIER_DIM), lambda bb, i: (bb, i, 0)),
        out_shape=jax.ShapeDtypeStruct((b, seg_rows, FOURIER_DIM), BF16),
        compiler_params=_cparams(2),
        name="token_dft",
    )(z, ct, st)


def _merge_kernel(h_ref, mod_ref, yf_ref, yb_ref, ro_ref, ya_ref, yd_ref, ug_ref, ln_ref, avg_ref,
                  wbr_ref, wba_ref, wbf_ref, wo_ref, gp_ref, o_ref):
    n = RWKV_DIM
    d = h_ref.shape[2]
    avg = avg_ref[...]
    y = yf_ref[0] + yb_ref[0]
    dev = y - _dot_exact_rhs(y, avg)
    var = _dot_exact_rhs(dev * dev, avg)
    yn = dev * lax.rsqrt(var + LNX_EPS) * ln_ref[0:1, :] + ln_ref[1:2, :]
    y_r = (yn + ro_ref[0, :, 0:n]) * ro_ref[0, :, n:2 * n]
    ug = ug_ref[0]
    mix = _sigmoid(ug[:, 0:d].astype(F32)) * _dot(y_r, wbr_ref[...])
    mix += _sigmoid(ug[:, d:2 * d].astype(F32)) * jnp.dot(ya_ref[0], wba_ref[...], preferred_element_type=F32)
    mix += _sigmoid(ug[:, 2 * d:3 * d].astype(F32)) * jnp.dot(yd_ref[0], wbf_ref[...], preferred_element_type=F32)
    o = _dot(mix, wo_ref[...])
    o = o * lax.rsqrt(jnp.mean(o * o, axis=-1, keepdims=True) + EPS) * gp_ref[...]
    o_ref[0] = h_ref[0] + mod_ref[0, 0, 2:3, :] * o


def _merge(h, modtab, yf, yb, ro, ya, yd, ug, p, n_rows, n_lat_tiles):
    b, _, d = h.shape
    tm = ROW_TILE
    row = lambda bb, i: (bb, i, 0)
    consts = [p["ln"], p["avg"], p["wbr"], p["wba"], p["wbf"], p["wo"], p["gpost"]]
    return pl.pallas_call(
        _merge_kernel,
        grid=(b, n_rows // tm),
        in_specs=[pl.BlockSpec((1, tm, d), row),
                  pl.BlockSpec((1, 1, 6, d), lambda bb, i: (bb, jnp.where(i < n_lat_tiles, 0, 1), 0, 0)),
                  pl.BlockSpec((1, tm, RWKV_DIM), row), pl.BlockSpec((1, tm, RWKV_DIM), row),
                  pl.BlockSpec((1, tm, 2 * RWKV_DIM), row),
                  pl.BlockSpec((1, tm, ATT_Q_DIM), row), pl.BlockSpec((1, tm, FOURIER_DIM), row),
                  pl.BlockSpec((1, tm, 3 * d), row)] + [_const_spec(x.shape) for x in consts],
        out_specs=pl.BlockSpec((1, tm, d), row),
        out_shape=jax.ShapeDtypeStruct((b, n_rows, d), F32),
        compiler_params=_cparams(2),
        name="branch_merge",
    )(h, modtab, yf, yb, ro, ya, yd, ug, *consts)


def _ffn_kernel(h_ref, hp_ref, hn_ref, mod_ref, gpre_ref, upg_ref, upv_ref, cw_ref, dn_ref, gpost_ref, o_ref,
                zg_ref, act_ref, *, n_lat_tiles, n_tiles):
    i = pl.program_id(1)
    tm = h_ref.shape[1]
    prev_ok = jnp.logical_and(i != 0, i != n_lat_tiles)
    next_ok = jnp.logical_and(i != n_lat_tiles - 1, i != n_tiles - 1)
    h = h_ref[0]
    x = jnp.concatenate([hp_ref[0], h, hn_ref[0]], axis=0)
    f = _rms_mod(x, gpre_ref[...], mod_ref[0, 0, 3:4, :], mod_ref[0, 0, 4:5, :])
    rid = lax.broadcasted_iota(jnp.int32, (tm + 2 * HALO, 1), 0)
    live = jnp.logical_and(jnp.logical_or(rid >= HALO, prev_ok), jnp.logical_or(rid < HALO + tm, next_ok))
    f_all = jnp.where(live, f, 0.0).astype(BF16)
    f_main = f[HALO:HALO + tm].astype(BF16)

    for c in range(zg_ref.shape[1] // FF_CHUNK):
        cols = slice(c * FF_CHUNK, (c + 1) * FF_CHUNK)
        zg_ref[:, cols] = jnp.dot(f_all, upg_ref[:, cols], preferred_element_type=F32)
        zv = jnp.dot(f_main, upv_ref[:, cols], preferred_element_type=F32)
        zg = (cw_ref[0:1, cols] * zg_ref[HALO - 1:HALO - 1 + tm, cols] + cw_ref[1:2, cols] * zg_ref[HALO:HALO + tm, cols]
              + cw_ref[2:3, cols] * zg_ref[HALO + 1:HALO + 1 + tm, cols] + cw_ref[3:4, cols])
        act = 0.5 * zg * (1.0 + jnp.tanh(0.7978845608028654 * (zg + 0.044715 * zg * zg * zg)))
        act_ref[:, cols] = (act * zv).astype(BF16)
    o = jnp.dot(act_ref[...], dn_ref[...], preferred_element_type=F32)
    o = o * lax.rsqrt(jnp.mean(o * o, axis=-1, keepdims=True) + EPS) * gpost_ref[...]
    o_ref[0] = h + mod_ref[0, 0, 5:6, :] * o


def _ffn(h, modtab, p, n_rows, n_lat_tiles):
    b, l, d = h.shape
    tm = ROW_TILE
    hb = tm // HALO
    n_tiles = l // tm
    d_ff = p["dn"].shape[0]
    row = lambda bb, i: (bb, i, 0)
    consts = [p["gpre"], p["upg"], p["upv"], p["cw"], p["dn"], p["gpost"]]
    return pl.pallas_call(
        functools.partial(_ffn_kernel, n_lat_tiles=n_lat_tiles, n_tiles=n_tiles),
        grid=(b, n_rows // tm),
        in_specs=[pl.BlockSpec((1, tm, d), row),
                  pl.BlockSpec((1, HALO, d), lambda bb, i: (bb, jnp.maximum(i * hb - 1, 0), 0)),
                  pl.BlockSpec((1, HALO, d), lambda bb, i: (bb, jnp.minimum((i + 1) * hb, l // HALO - 1), 0)),
                  pl.BlockSpec((1, 1, 6, d), lambda bb, i: (bb, jnp.where(i < n_lat_tiles, 0, 1), 0, 0))]
        + [_const_spec(x.shape) for x in consts],
        out_specs=pl.BlockSpec((1, tm, d), row),
        out_shape=jax.ShapeDtypeStruct((b, n_rows, d), F32),
        scratch_shapes=[pltpu.VMEM((tm + 2 * HALO, d_ff), F32), pltpu.VMEM((tm, d_ff), BF16)],
        compiler_params=_cparams(2),
        name="conv_ffn",
    )(h, h, h, modtab, *consts)


def _rope_tables(n_lat, n_ctx):
    t = jnp.arange(n_lat)
    row_id = (t // GRID_W).astype(F32)
    col_id = (t % GRID_W).astype(F32)
    inv = ROPE_BASE ** (-jnp.arange(ROPE_FREQS, dtype=F32) / ROPE_FREQS)
    d = np.arange(ATT_HEAD)
    freq = d % ROPE_FREQS
    ang = jnp.where((d // (2 * ROPE_FREQS) == 0)[None, :], row_id[:, None], col_id[:, None]) * inv[freq][None, :]
    sign = np.where((d // ROPE_FREQS) % 2 == 0, -1.0, 1.0).astype(np.float32)
    cos = jnp.concatenate([jnp.cos(ang), jnp.ones((n_ctx, ATT_HEAD), F32)], axis=0)
    sin = jnp.concatenate([jnp.sin(ang) * sign[None, :], jnp.zeros((n_ctx, ATT_HEAD), F32)], axis=0)
    return jnp.tile(cos, (1, ATT_HEADS)), jnp.tile(sin, (1, ATT_HEADS))


def _dft_mats(n):
    idx = jnp.arange(n, dtype=jnp.int32)
    ang = ((idx[:, None] * idx[None, :]) % n).astype(F32) * (2.0 * math.pi / n)
    scale = 1.0 / math.sqrt(n)
    return jnp.cos(ang) * scale, jnp.sin(ang) * scale


def _block_diag(m, groups):
    return jnp.kron(jnp.eye(groups, dtype=m.dtype), m)


def kernel(x, c, ctx, c_ctx, mod_w, mod_b, norm_mix_pre, norm_mix_post, norm_ffn_pre, norm_ffn_post, w_in, rwkv_mu, rwkv_w0, rwkv_w2, rwkv_a0, rwkv_a2, rwkv_g2, rwkv_k_k, rwkv_k_a, rwkv_r_k, rwkv_lnx_w, rwkv_lnx_b, rwkv_v0, rwkv_v1, rwkv_v2, attn_sink, w_branch_rwkv, w_branch_attn, w_branch_fourier, w_out, ffn_up, ffn_conv_w, ffn_conv_b, ffn_down):
    b, n_lat, d = x.shape
    n_ctx = ctx.shape[1]
    depth = mod_w.shape[0]
    d_ff = ffn_down.shape[1]
    l = n_lat + n_ctx
    tm = ROW_TILE
    assert n_lat % tm == 0 and n_ctx % tm == 0 and n_lat % n_ctx == 0
    assert n_lat >= 3 * ATT_BLOCK and d_ff % FF_CHUNK == 0 and b % WKV_BATCH == 0
    n_lat_tiles = n_lat // tm

    cos_t, sin_t = _rope_tables(n_lat, n_ctx)
    cg, sg = _dft_mats(FOURIER_GROUP_DIM)
    dft_c = jnp.concatenate([_block_diag(cg, FOURIER_GROUPS), _block_diag(sg, FOURIER_GROUPS)], axis=1).astype(BF16)
    ct_lat, st_lat = (m.astype(BF16) for m in _dft_mats(n_lat))
    ct_ctx, st_ctx = (m.astype(BF16) for m in _dft_mats(n_ctx))
    seg = _block_diag(jnp.ones((RWKV_HEAD, RWKV_HEAD), F32), RWKV_HEADS).astype(BF16)
    avg = (seg.astype(F32) / RWKV_HEAD).astype(BF16)

    pad = (-(b + 1)) % 8
    cvec = jnp.concatenate([c, c_ctx[None, :], jnp.zeros((pad, d), F32)], axis=0)
    mod = _modulation(cvec, mod_w, mod_b)

    h = jnp.concatenate([x, ctx], axis=1)
    v_first = None
    for layer in range(depth):
        last = layer == depth - 1
        lat = mod[layer, :b].reshape(b, 1, 6, d)
        cm = jnp.broadcast_to(mod[layer, b].reshape(1, 1, 6, d), (b, 1, 6, d))
        modtab = jnp.concatenate([lat, cm], axis=1)

        u_r, q, kv, z, u_g = _inproj(h, modtab, norm_mix_pre[layer], w_in[layer], dft_c, cos_t, sin_t, n_lat_tiles)

        fp = {
            "mu": rwkv_mu[layer],
            "vec": jnp.stack([rwkv_k_k[layer], rwkv_k_a[layer], rwkv_r_k[layer]]),
            "w0": rwkv_w0[layer], "a0": rwkv_a0[layer],
            "w2": rwkv_w2[layer].astype(BF16), "a2": rwkv_a2[layer].astype(BF16),
            "g2": rwkv_g2[layer].astype(BF16), "seg": seg,
        }
        if layer > 0:
            lp = 128 - MV_LORA
            fp["v0"] = rwkv_v0[layer - 1].reshape(1, RWKV_DIM)
            fp["v1"] = jnp.pad(rwkv_v1[layer - 1], ((0, 0), (0, lp))).astype(BF16)
            fp["v2"] = jnp.pad(rwkv_v2[layer - 1], ((0, lp), (0, 0))).astype(BF16)
        fc, ff, fb, ro = _rwkv_features(u_r, v_first, fp, n_lat_tiles)
        if layer == 0:
            v_first = fc
        y_fwd, y_bwd = _wkv_scan(fc, ff, fb, n_lat)

        n_rows = n_lat if last else l
        y_att = _attention(q, kv, attn_sink[layer], n_lat, n_rows)
        y_dft = _token_dft(z, ct_lat, st_lat, n_lat, 0)
        if not last:
            y_dft = jnp.concatenate([y_dft, _token_dft(z, ct_ctx, st_ctx, n_ctx, n_lat // n_ctx)], axis=1)

        mp = {
            "ln": jnp.stack([rwkv_lnx_w[layer], rwkv_lnx_b[layer]]), "avg": avg,
            "wbr": w_branch_rwkv[layer].astype(BF16), "wba": w_branch_attn[layer].astype(BF16),
            "wbf": w_branch_fourier[layer].astype(BF16), "wo": w_out[layer].astype(BF16),
            "gpost": norm_mix_post[layer].reshape(1, d),
        }
        h = _merge(h, modtab, y_fwd, y_bwd, ro, y_att, y_dft, u_g, mp, n_rows, n_lat_tiles)

        up = ffn_up[layer].astype(BF16)
        pp = {
            "gpre": norm_ffn_pre[layer].reshape(1, d),
            "upg": up[:, :d_ff], "upv": up[:, d_ff:],
            "cw": jnp.concatenate([ffn_conv_w[layer], ffn_conv_b[layer][None, :]], axis=0),
            "dn": ffn_down[layer].astype(BF16),
            "gpost": norm_ffn_post[layer].reshape(1, d),
        }
        h = _ffn(h, modtab, pp, n_rows, n_lat_tiles)
    return h
```

```python
import functools
import math

import numpy as np
import jax
import jax.numpy as jnp
from jax import lax
from jax.experimental import pallas as pl
from jax.experimental.pallas import tpu as pltpu

F32 = jnp.float32
BF16 = jnp.bfloat16

GRID_W = 64
RWKV_HEADS = 4
RWKV_HEAD = 64
RWKV_DIM = RWKV_HEADS * RWKV_HEAD
DECAY_LORA = 64
AAA_LORA = 64
MV_LORA = 32
GATE_LORA = 128
LNX_EPS = 64e-5
ATT_HEADS = 8
ATT_KV_HEADS = 2
ATT_GROUP = ATT_HEADS // ATT_KV_HEADS
ATT_HEAD = 64
ATT_Q_DIM = ATT_HEADS * ATT_HEAD
ATT_KV_DIM = ATT_KV_HEADS * ATT_HEAD
ATT_SCALE = ATT_HEAD ** -0.5
WINDOW = 128
ROPE_BASE = 10000.0
ROPE_FREQS = ATT_HEAD // 4
NEG_INF = -1e30
FOURIER_GROUPS = 4
FOURIER_GROUP_DIM = 64
FOURIER_DIM = FOURIER_GROUPS * FOURIER_GROUP_DIM
EPS = 1e-6
RWKV_COLS = 3 * RWKV_DIM + 2 * DECAY_LORA + 2 * AAA_LORA + GATE_LORA

ROW_TILE = 256
ROW_BATCH = 2
HALO = 8
WKV_CHUNK = 64
WKV_BATCH = 4
ATT_BLOCK = 128
FF_CHUNK = 256
VMEM_LIMIT = 56 * 1024 * 1024


def _cparams(n_axes):
    return pltpu.CompilerParams(dimension_semantics=("arbitrary",) * n_axes,
                                vmem_limit_bytes=VMEM_LIMIT)


def _dot(a, b):
    return jnp.dot(a.astype(BF16), b.astype(BF16), preferred_element_type=F32)


def _dot_nt(a, b):
    return lax.dot_general(a.astype(BF16), b.astype(BF16), (((1,), (1,)), ((), ())),
                           preferred_element_type=F32)


def _split3(x):
    hi = x.astype(BF16)
    r1 = x - hi.astype(F32)
    mid = r1.astype(BF16)
    lo = (r1 - mid.astype(F32)).astype(BF16)
    return hi, mid, lo


def _dot_exact_rhs(x, m):
    hi, mid, lo = _split3(x)
    return (jnp.dot(hi, m, preferred_element_type=F32) + jnp.dot(mid, m, preferred_element_type=F32)
            + jnp.dot(lo, m, preferred_element_type=F32))


def _dot_exact_lhs(m, x):
    hi, mid, lo = _split3(x)
    return (jnp.dot(m, hi, preferred_element_type=F32) + jnp.dot(m, mid, preferred_element_type=F32)
            + jnp.dot(m, lo, preferred_element_type=F32))


def _sigmoid(x):
    return 1.0 / (1.0 + jnp.exp(-x))


def _const_spec(shape):
    nd = len(shape)
    return pl.BlockSpec(shape, lambda *_: (0,) * nd, pipeline_mode=pl.Buffered(1))


def _rms_mod(x, g, shift, scale):
    y = x * lax.rsqrt(jnp.mean(x * x, axis=-1, keepdims=True) + EPS) * g
    return y * (1.0 + scale) + shift


def _mod_kernel(c_ref, w_ref, b_ref, o_ref):
    x = c_ref[...]
    o_ref[0] = _dot(x * _sigmoid(x), w_ref[0]) + b_ref[0]


def _modulation(cvec, mod_w, mod_b):
    depth, d, n = mod_w.shape
    rows = cvec.shape[0]
    tn = 1536
    return pl.pallas_call(
        _mod_kernel,
        grid=(depth, n // tn),
        in_specs=[pl.BlockSpec((rows, d), lambda l, j: (0, 0)),
                  pl.BlockSpec((1, d, tn), lambda l, j: (l, 0, j)),
                  pl.BlockSpec((1, 1, tn), lambda l, j: (l, 0, j))],
        out_specs=pl.BlockSpec((1, rows, tn), lambda l, j: (l, 0, j)),
        out_shape=jax.ShapeDtypeStruct((depth, rows, n), F32),
        compiler_params=_cparams(2),
        name="adaln_modulation",
    )(cvec, mod_w.astype(BF16), mod_b.reshape(depth, 1, n))


def _rope(x, cos, sin_signed):
    n = x.shape[1]
    lane = lax.broadcasted_iota(jnp.int32, x.shape, 1)
    first = (lane & ROPE_FREQS) == 0
    partner = jnp.where(first, pltpu.roll(x, n - ROPE_FREQS, 1), pltpu.roll(x, ROPE_FREQS, 1))
    return x * cos + partner * sin_signed


def _inproj_kernel(*refs, has_vres, n_lat_tiles, n_tiles):
    (h_ref, hp_ref, hn_ref, mod_ref, g_ref, wr_ref, wq_ref, wkv_ref, wf_ref, wg_ref, dft_ref, cos_ref, sin_ref,
     mu_ref, vec_ref, w0_ref, a0_ref, w2_ref, a2_ref, g2_ref, seg_ref) = refs[:21]
    refs = refs[21:]
    if has_vres:
        vf_ref, v0_ref, v1_ref, v2_ref = refs[:4]
        refs = refs[4:]
    q_ref, kv_ref, z_ref, ug_ref, fc_ref, ff_ref, fb_ref, ro_ref, buf_ref = refs
    i = pl.program_id(1)
    nb, tm, _ = h_ref.shape
    th = tm + 2 * HALO
    prev_ok = jnp.logical_and(i != 0, i != n_lat_tiles)
    next_ok = jnp.logical_and(i != n_lat_tiles - 1, i != n_tiles - 1)
    rid = lax.broadcasted_iota(jnp.int32, (th, 1), 0)
    live = jnp.logical_and(jnp.logical_or(rid >= HALO, prev_ok), jnp.logical_or(rid < HALO + tm, next_ok))
    a_all, a_main = [], []
    for bi in range(nb):
        x = jnp.concatenate([hp_ref[bi], h_ref[bi], hn_ref[bi]], axis=0)
        a = _rms_mod(x, g_ref[...], mod_ref[bi, 0, 0:1, :], mod_ref[bi, 0, 1:2, :])
        a_all.append(jnp.where(live, a, 0.0).astype(BF16))
        a_main.append(a[HALO:HALO + tm].astype(BF16))
    a_all = jnp.concatenate(a_all, axis=0)
    a = jnp.concatenate(a_main, axis=0)
    cos = jnp.concatenate([cos_ref[...]] * nb, axis=0)
    sin = jnp.concatenate([sin_ref[...]] * nb, axis=0)
    buf_ref[...] = jnp.dot(a_all, wr_ref[...], preferred_element_type=F32).reshape(nb, th, -1)
    q = jnp.dot(a, wq_ref[...], preferred_element_type=F32)
    q_ref[...] = (_rope(q, cos, sin) * ATT_SCALE).astype(BF16).reshape(nb, tm, -1)
    kv = jnp.dot(a, wkv_ref[...], preferred_element_type=F32)
    k = _rope(kv[:, :ATT_KV_DIM], cos[:, :ATT_KV_DIM], sin[:, :ATT_KV_DIM])
    kv_ref[:, :, :ATT_KV_DIM] = k.astype(BF16).reshape(nb, tm, -1)
    kv_ref[:, :, ATT_KV_DIM:] = kv[:, ATT_KV_DIM:].astype(BF16).reshape(nb, tm, -1)
    uf = jnp.dot(a, wf_ref[...], preferred_element_type=F32)
    z_ref[...] = _dot(uf, dft_ref[...]).astype(BF16).reshape(nb, tm, -1)
    half = wg_ref.shape[1] // 2
    ug_ref[:, :, :half] = jnp.dot(a, wg_ref[:, :half], preferred_element_type=F32).astype(BF16).reshape(nb, tm, -1)
    _rwkv_features(buf_ref, tm, vf_ref if has_vres else None, (v0_ref, v1_ref, v2_ref) if has_vres else None,
                   mu_ref, vec_ref, w0_ref, a0_ref, w2_ref, a2_ref, g2_ref, seg_ref, fc_ref, ff_ref, fb_ref, ro_ref)
    ug_ref[:, :, half:] = jnp.dot(a, wg_ref[:, half:], preferred_element_type=F32).astype(BF16).reshape(nb, tm, -1)


def _inproj(h, modtab, g, w_in, dft_c, cos_t, sin_t, v_first_src, p, n_lat_tiles):
    b, l, d = h.shape
    tm = ROW_TILE
    n = RWKV_DIM
    n_tiles = l // tm
    hb = tm // HALO
    n_g = w_in.shape[1] - (RWKV_COLS + ATT_Q_DIM + 2 * ATT_KV_DIM + FOURIER_DIM)
    o = np.cumsum([0, RWKV_COLS, ATT_Q_DIM, 2 * ATT_KV_DIM, FOURIER_DIM, n_g])
    wb = w_in.astype(BF16)
    ws = [wb[:, o[i]:o[i + 1]] for i in range(5)]
    nb = ROW_BATCH
    has_vres = v_first_src is not None
    row = lambda bb, i: (bb, i, 0)
    consts = [g.reshape(1, d)] + ws + [dft_c]
    feat_consts = [p["mu"], p["vec"], p["w0"], p["a0"], p["w2"], p["a2"], p["g2"], p["seg"]]
    in_specs = ([pl.BlockSpec((nb, tm, d), row),
                 pl.BlockSpec((nb, HALO, d), lambda bb, i: (bb, jnp.maximum(i * hb - 1, 0), 0)),
                 pl.BlockSpec((nb, HALO, d), lambda bb, i: (bb, jnp.minimum((i + 1) * hb, l // HALO - 1), 0)),
                 pl.BlockSpec((nb, 1, 6, d), lambda bb, i: (bb, jnp.where(i < n_lat_tiles, 0, 1), 0, 0))]
                + [_const_spec(x.shape) for x in consts]
                + [pl.BlockSpec((tm, ATT_Q_DIM), lambda bb, i: (i, 0)),
                   pl.BlockSpec((tm, ATT_Q_DIM), lambda bb, i: (i, 0))]
                + [_const_spec(x.shape) for x in feat_consts])
    args = [h, h, h, modtab] + consts + [cos_t, sin_t] + feat_consts
    if has_vres:
        vres_consts = [p["v0"], p["v1"], p["v2"]]
        in_specs += [pl.BlockSpec((nb, tm, n), lambda bb, i: (bb, i, 1))] + [_const_spec(x.shape) for x in vres_consts]
        args += [v_first_src] + vres_consts
    outs = [(ATT_Q_DIM, BF16), (2 * ATT_KV_DIM, BF16), (2 * FOURIER_DIM, BF16), (n_g, BF16),
            (3 * n, F32), (3 * n, F32), (3 * n, F32), (2 * n, F32)]
    return pl.pallas_call(
        functools.partial(_inproj_kernel, has_vres=has_vres, n_lat_tiles=n_lat_tiles, n_tiles=n_tiles),
        grid=(b // nb, n_tiles),
        in_specs=in_specs,
        out_specs=[pl.BlockSpec((nb, tm, w), row) for w, _ in outs],
        out_shape=[jax.ShapeDtypeStruct((b, l, w), dt) for w, dt in outs],
        scratch_shapes=[pltpu.VMEM((nb, tm + 2 * HALO, RWKV_COLS), F32)],
        compiler_params=_cparams(2),
        name="in_projection",
    )(*args)


def _rwkv_features(buf_ref, tm, vf_ref, vres, mu_ref, vec_ref, w0_ref, a0_ref, w2_ref, a2_ref, g2_ref, seg_ref,
                   fc_ref, ff_ref, fb_ref, ro_ref):
    n = RWKV_DIM
    k_k = vec_ref[0:1, :]
    k_a = vec_ref[1:2, :]
    r_k = vec_ref[2:3, :]
    seg = seg_ref[...]
    tiles = []
    for bi in range(buf_ref.shape[0]):
        u = buf_ref[bi, HALO:HALO + tm, :]
        u_prev = buf_ref[bi, HALO - 1:HALO - 1 + tm, :]
        u_next = buf_ref[bi, HALO + 1:HALO + 1 + tm, :]
        us = u + mu_ref[0:1, :] * (u_prev - u) + mu_ref[1:2, :] * (u_next - u)
        o = 3 * n
        t = dict(bi=bi, r=us[:, 0:n], k=us[:, n:2 * n], v=us[:, 2 * n:3 * n])
        t["tw"] = [jnp.tanh(us[:, o + d * DECAY_LORA:o + (d + 1) * DECAY_LORA]).astype(BF16) for d in range(2)]
        o += 2 * DECAY_LORA
        t["xa"] = [us[:, o + d * AAA_LORA:o + (d + 1) * AAA_LORA].astype(BF16) for d in range(2)]
        o += 2 * AAA_LORA
        t["sg"] = _sigmoid(us[:, o:o + GATE_LORA]).astype(BF16)
        t["kk"] = t["k"] * k_k
        t["kk_sq"] = _split3(t["kk"] * t["kk"])
        tiles.append(t)
    for t in tiles:
        t["z"] = [jnp.dot(t["tw"][d], w2_ref[d], preferred_element_type=F32) for d in range(2)]
        t["za"] = [jnp.dot(t["xa"][d], a2_ref[d], preferred_element_type=F32) for d in range(2)]
        t["g"] = jnp.dot(t["sg"], g2_ref[...], preferred_element_type=F32)
        t["ss"] = sum(jnp.dot(part, seg, preferred_element_type=F32) for part in t["kk_sq"])
        if vres is not None:
            t["vv"] = _dot(t["v"], vres[1][...])
    if vres is not None:
        for t in tiles:
            t["vg"] = _dot(t["vv"], vres[2][...])
    for t in tiles:
        bi = t["bi"]
        v = t["v"]
        if vres is not None:
            v = v + (vf_ref[bi] - v) * _sigmoid(vres[0][...] + t["vg"])
        kk = t["kk"] * lax.rsqrt(jnp.maximum(t["ss"], 1e-24))
        keys = []
        for d, out_ref in enumerate((ff_ref, fb_ref)):
            z = w0_ref[d:d + 1, :] + t["z"][d]
            softplus = jnp.maximum(-z, 0.0) + jnp.log(1.0 + jnp.exp(-jnp.abs(z)))
            out_ref[bi, :, 0:n] = -jnp.exp(-softplus - 0.5)
            a = _sigmoid(a0_ref[d:d + 1, :] + t["za"][d])
            key = t["k"] * (1.0 + (a - 1.0) * k_a)
            keys.append(key)
            out_ref[bi, :, n:2 * n] = key
            out_ref[bi, :, 2 * n:3 * n] = kk * a
        t["v"] = v
        t["bonus_in"] = _split3(t["r"] * (0.5 * (keys[0] + keys[1])) * r_k)
        fc_ref[bi, :, 0:n] = t["r"]
        fc_ref[bi, :, n:2 * n] = v
        fc_ref[bi, :, 2 * n:3 * n] = kk
        ro_ref[bi, :, n:2 * n] = t["g"]
    for t in tiles:
        bonus = sum(jnp.dot(part, seg, preferred_element_type=F32) for part in t["bonus_in"])
        ro_ref[t["bi"], :, 0:n] = bonus * t["v"]


def _wkv_masks(c, reverse):
    ti = lax.broadcasted_iota(jnp.int32, (c, c), 0)
    si = lax.broadcasted_iota(jnp.int32, (c, c), 1)
    incl, strict = (si >= ti, si > ti) if reverse else (si <= ti, si < ti)
    levels = []
    for sh in range(int(math.log2(c))):
        bt = lax.shift_right_logical(ti, sh)
        bs = lax.shift_right_logical(si, sh)
        if reverse:
            levels.append(jnp.logical_and((bt & 1) == 0, bs == bt + 1))
        else:
            levels.append(jnp.logical_and((bt & 1) == 1, bs == bt - 1))
    return incl, strict, levels, jnp.where(ti == si, 1.0, 0.0)


def _wkv_kernel(fcf_ref, ff_ref, fcb_ref, fb_ref, yf_ref, yb_ref, sf_ref, sb_ref):
    @pl.when(pl.program_id(1) == 0)
    def _():
        sf_ref[...] = jnp.zeros_like(sf_ref)
        sb_ref[...] = jnp.zeros_like(sb_ref)

    c = WKV_CHUNK
    n = RWKV_DIM
    hd = RWKV_HEAD
    nb = fcf_ref.shape[0]
    probs = []
    for fc_ref, fd_ref, y_ref, s_ref, reverse in ((fcf_ref, ff_ref, yf_ref, sf_ref, False),
                                                  (fcb_ref, fb_ref, yb_ref, sb_ref, True)):
        incl, strict, levels, eye = _wkv_masks(c, reverse)
        tri = jnp.where(incl, 1.0, 0.0).astype(BF16)
        cums = [_dot_exact_lhs(tri, fd_ref[bi, :, 0:n]) for bi in range(nb)]
        for bi in range(nb):
            cum = cums[bi]
            total = cum[0:1, :] if reverse else cum[c - 1:c, :]
            c0 = 0.5 * total
            e_neg = jnp.exp(c0 - cum)
            a_t = -fc_ref[bi, :, 2 * n:3 * n] * jnp.exp(cum - fd_ref[bi, :, 0:n] - c0)
            r_t = fc_ref[bi, :, 0:n] * jnp.exp(cum - c0)
            b_t = fd_ref[bi, :, 2 * n:3 * n] * e_neg
            k_t = fd_ref[bi, :, n:2 * n] * e_neg
            e_half = jnp.exp(c0)
            e_tot = jnp.exp(total)
            for h in range(RWKV_HEADS):
                sl = slice(h * hd, (h + 1) * hd)
                probs.append(dict(
                    lhs=jnp.concatenate([a_t[:, sl], r_t[:, sl]], axis=0).astype(BF16),
                    rhs=jnp.concatenate([b_t[:, sl], k_t[:, sl]], axis=0).astype(BF16),
                    v=fc_ref[bi, :, n + h * hd:n + (h + 1) * hd], s0=s_ref[bi, h],
                    e_half=e_half[:, sl], e_tot=e_tot[:, sl], incl=incl, strict=strict, levels=levels, eye=eye,
                    y_ref=y_ref, s_ref=s_ref, bi=bi, h=h, sl=sl))

    for p in probs:
        p["g"] = _dot_nt(p["lhs"], p["rhs"])
    for p in probs:
        p["a_s"] = _dot_nt(p["lhs"], p["s0"] * p["e_half"])
    for p in probs:
        g = p["g"]
        p["a_ab"] = jnp.where(p["strict"], g[:c, :c], 0.0)
        a_ak = jnp.where(p["strict"], g[:c, c:], 0.0)
        p["a_r"] = jnp.concatenate([jnp.where(p["incl"], g[c:, :c], 0.0),
                                    jnp.where(p["incl"], g[c:, c:], 0.0)], axis=1).astype(BF16)
        p["t"] = p["eye"] + jnp.where(p["levels"][0], p["a_ab"], 0.0)
        p["rhs_u"] = p["a_s"][:c] + _dot(a_ak, p["v"])
    for lvl in range(1, len(probs[0]["levels"])):
        for p in probs:
            p["tb"] = p["t"].astype(BF16)
            p["tmp"] = _dot(jnp.where(p["levels"][lvl], p["a_ab"], 0.0), p["tb"])
        for p in probs:
            p["t"] = p["t"] + _dot(p["tb"], p["tmp"])
    for p in probs:
        p["uv"] = jnp.concatenate([_dot(p["t"], p["rhs_u"]), p["v"]], axis=0).astype(BF16)
    for p in probs:
        p["y_ref"][p["bi"], :, p["sl"]] = p["a_s"][c:] + _dot(p["a_r"], p["uv"])
    for p in probs:
        upd = lax.dot_general(p["uv"], p["rhs"], (((0,), (0,)), ((), ())), preferred_element_type=F32)
        p["s_ref"][p["bi"], p["h"]] = p["s0"] * p["e_tot"] + upd * p["e_half"]


def _wkv_scan(fc, ff, fb, n_lat):
    b, l, w = fc.shape
    c = WKV_CHUNK
    nb = WKV_BATCH
    nl = n_lat // c
    nc = l // c - nl
    fwd = lambda bb, j: (bb, jnp.where(j < nc, nl + j, j - nc), 0)
    bwd = lambda bb, j: (bb, nl + nc - 1 - j, 0)
    return pl.pallas_call(
        _wkv_kernel,
        grid=(b // nb, nl + nc),
        in_specs=[pl.BlockSpec((nb, c, w), fwd), pl.BlockSpec((nb, c, w), fwd),
                  pl.BlockSpec((nb, c, w), bwd), pl.BlockSpec((nb, c, w), bwd)],
        out_specs=[pl.BlockSpec((nb, c, RWKV_DIM), fwd), pl.BlockSpec((nb, c, RWKV_DIM), bwd)],
        out_shape=[jax.ShapeDtypeStruct((b, l, RWKV_DIM), F32)] * 2,
        scratch_shapes=[pltpu.VMEM((nb, RWKV_HEADS, RWKV_HEAD, RWKV_HEAD), F32)] * 2,
        compiler_params=_cparams(2),
        name="wkv_scan",
    )(fc, ff, fc, fb)


def _attn_kernel(sink_ref, q_ref, kv_ref, o_ref, *, n_lat):
    j = pl.program_id(1)
    qb = ATT_BLOCK
    hd = ATT_HEAD
    n_ctx = kv_ref.shape[1] - n_lat
    n_win = 3 * qb
    is_lat = j * qb < n_lat
    ws = pl.multiple_of(jnp.clip((j - 1) * qb, 0, n_lat - n_win), qb)
    col = lax.broadcasted_iota(jnp.int32, (qb, n_ctx + n_win), 1)
    q_pos = j * qb + lax.broadcasted_iota(jnp.int32, (qb, n_ctx + n_win), 0)
    k_pos = ws + col - n_ctx
    valid = jnp.logical_or(col < n_ctx, jnp.logical_and(jnp.abs(q_pos - k_pos) <= WINDOW, is_lat))
    q = q_ref[0]
    kv = jnp.concatenate([kv_ref[0, n_lat:n_lat + n_ctx, :], kv_ref[0, pl.ds(ws, n_win), :]], axis=0)
    heads = range(ATT_HEADS)
    s = [jnp.where(valid, _dot_nt(q[:, h * hd:(h + 1) * hd],
                                  kv[:, (h // ATT_GROUP) * hd:(h // ATT_GROUP + 1) * hd]), NEG_INF) for h in heads]
    m = [jnp.maximum(jnp.max(s[h], axis=-1, keepdims=True), sink_ref[h]) for h in heads]
    p = [jnp.exp(s[h] - m[h]) for h in heads]
    den = [jnp.sum(p[h], axis=-1, keepdims=True) + jnp.exp(sink_ref[h] - m[h]) for h in heads]
    o = [_dot(p[h], kv[:, ATT_KV_DIM + (h // ATT_GROUP) * hd:ATT_KV_DIM + (h // ATT_GROUP + 1) * hd]) for h in heads]
    for h in heads:
        o_ref[0, :, h * hd:(h + 1) * hd] = (o[h] / den[h]).astype(o_ref.dtype)


def _attention(q, kv, sink, n_lat, n_rows):
    b, l, _ = q.shape
    qb = ATT_BLOCK
    return pl.pallas_call(
        functools.partial(_attn_kernel, n_lat=n_lat),
        grid=(b, n_rows // qb),
        in_specs=[pl.BlockSpec(memory_space=pltpu.SMEM),
                  pl.BlockSpec((1, qb, ATT_Q_DIM), lambda bb, j: (bb, j, 0)),
                  pl.BlockSpec((1, l, kv.shape[2]), lambda bb, j: (bb, 0, 0))],
        out_specs=pl.BlockSpec((1, qb, ATT_Q_DIM), lambda bb, j: (bb, j, 0)),
        out_shape=jax.ShapeDtypeStruct((b, n_rows, ATT_Q_DIM), BF16),
        compiler_params=_cparams(2),
        name="windowed_attention",
    )(sink, q, kv)


def _dft_kernel(z_ref, ct_ref, st_ref, o_ref):
    n = FOURIER_DIM
    o_ref[0] = (jnp.dot(ct_ref[...], z_ref[0, :, 0:n], preferred_element_type=F32)
                - jnp.dot(st_ref[...], z_ref[0, :, n:2 * n], preferred_element_type=F32)).astype(o_ref.dtype)


def _token_dft(z, ct, st, seg_rows, seg_block):
    b = z.shape[0]
    return pl.pallas_call(
        _dft_kernel,
        grid=(b,),
        in_specs=[pl.BlockSpec((1, seg_rows, z.shape[2]), lambda bb: (bb, seg_block, 0)),
                  _const_spec(ct.shape), _const_spec(st.shape)],
        out_specs=pl.BlockSpec((1, seg_rows, FOURIER_DIM), lambda bb: (bb, 0, 0)),
        out_shape=jax.ShapeDtypeStruct((b, seg_rows, FOURIER_DIM), BF16),
        compiler_params=_cparams(1),
        name="token_dft",
    )(z, ct, st)


def _merge_kernel(h_ref, mod_ref, yf_ref, yb_ref, ro_ref, ya_ref, yd_ref, ug_ref, ln_ref, avg_ref,
                  wbr_ref, wba_ref, wbf_ref, wo_ref, gp_ref, o_ref):
    n = RWKV_DIM
    d = h_ref.shape[2]
    avg = avg_ref[...]
    nb, tm, _ = h_ref.shape
    rows = nb * tm
    y = (yf_ref[...] + yb_ref[...]).reshape(rows, n)
    dev = y - _dot_exact_rhs(y, avg)
    var = _dot_exact_rhs(dev * dev, avg)
    yn = dev * lax.rsqrt(var + LNX_EPS) * ln_ref[0:1, :] + ln_ref[1:2, :]
    ro = ro_ref[...].reshape(rows, 2 * n)
    y_r = (yn + ro[:, 0:n]) * ro[:, n:2 * n]
    mix = _sigmoid(ug_ref[:, :, 0:d].reshape(rows, d).astype(F32)) * _dot(y_r, wbr_ref[...])
    mix += (_sigmoid(ug_ref[:, :, d:2 * d].reshape(rows, d).astype(F32))
            * jnp.dot(ya_ref[...].reshape(rows, -1), wba_ref[...], preferred_element_type=F32))
    mix += (_sigmoid(ug_ref[:, :, 2 * d:3 * d].reshape(rows, d).astype(F32))
            * jnp.dot(yd_ref[...].reshape(rows, -1), wbf_ref[...], preferred_element_type=F32))
    o = _dot(mix, wo_ref[...])
    o = o * lax.rsqrt(jnp.mean(o * o, axis=-1, keepdims=True) + EPS) * gp_ref[...]
    for bi in range(nb):
        o_ref[bi] = h_ref[bi] + mod_ref[bi, 0, 2:3, :] * o[bi * tm:(bi + 1) * tm]


def _merge(h, modtab, yf, yb, ro, ya, yd, ug, p, n_rows, n_lat_tiles):
    b, _, d = h.shape
    tm = ROW_TILE
    nb = ROW_BATCH
    row = lambda bb, i: (bb, i, 0)
    consts = [p["ln"], p["avg"], p["wbr"], p["wba"], p["wbf"], p["wo"], p["gpost"]]
    return pl.pallas_call(
        _merge_kernel,
        grid=(b // nb, n_rows // tm),
        in_specs=[pl.BlockSpec((nb, tm, d), row),
                  pl.BlockSpec((nb, 1, 6, d), lambda bb, i: (bb, jnp.where(i < n_lat_tiles, 0, 1), 0, 0)),
                  pl.BlockSpec((nb, tm, RWKV_DIM), row), pl.BlockSpec((nb, tm, RWKV_DIM), row),
                  pl.BlockSpec((nb, tm, 2 * RWKV_DIM), row),
                  pl.BlockSpec((nb, tm, ATT_Q_DIM), row), pl.BlockSpec((nb, tm, FOURIER_DIM), row),
                  pl.BlockSpec((nb, tm, 3 * d), row)] + [_const_spec(x.shape) for x in consts],
        out_specs=pl.BlockSpec((nb, tm, d), row),
        out_shape=jax.ShapeDtypeStruct((b, n_rows, d), F32),
        compiler_params=_cparams(2),
        name="branch_merge",
    )(h, modtab, yf, yb, ro, ya, yd, ug, *consts)


def _ffn_kernel(h_ref, hp_ref, hn_ref, mod_ref, gpre_ref, upg_ref, upv_ref, cw_ref, dn_ref, gpost_ref, o_ref,
                zg_ref, act_ref, *, n_lat_tiles, n_tiles):
    i = pl.program_id(1)
    tm = h_ref.shape[1]
    prev_ok = jnp.logical_and(i != 0, i != n_lat_tiles)
    next_ok = jnp.logical_and(i != n_lat_tiles - 1, i != n_tiles - 1)
    rid = lax.broadcasted_iota(jnp.int32, (tm + 2 * HALO, 1), 0)
    live = jnp.logical_and(jnp.logical_or(rid >= HALO, prev_ok), jnp.logical_or(rid < HALO + tm, next_ok))
    nb = h_ref.shape[0]
    th = tm + 2 * HALO
    f_all, f_main = [], []
    for bi in range(nb):
        x = jnp.concatenate([hp_ref[bi], h_ref[bi], hn_ref[bi]], axis=0)
        f = _rms_mod(x, gpre_ref[...], mod_ref[bi, 0, 3:4, :], mod_ref[bi, 0, 4:5, :])
        f_all.append(jnp.where(live, f, 0.0).astype(BF16))
        f_main.append(f[HALO:HALO + tm].astype(BF16))
    f_all = jnp.concatenate(f_all, axis=0)
    f_main = jnp.concatenate(f_main, axis=0)
    for c in range(zg_ref.shape[2] // FF_CHUNK):
        cols = slice(c * FF_CHUNK, (c + 1) * FF_CHUNK)
        zg_ref[:, :, cols] = jnp.dot(f_all, upg_ref[:, cols], preferred_element_type=F32).reshape(nb, th, -1)
        zv = jnp.dot(f_main, upv_ref[:, cols], preferred_element_type=F32).reshape(nb, tm, -1)
        zg = (cw_ref[0:1, cols] * zg_ref[:, HALO - 1:HALO - 1 + tm, cols]
              + cw_ref[1:2, cols] * zg_ref[:, HALO:HALO + tm, cols]
              + cw_ref[2:3, cols] * zg_ref[:, HALO + 1:HALO + 1 + tm, cols] + cw_ref[3:4, cols])
        act = 0.5 * zg * (1.0 + jnp.tanh(0.7978845608028654 * (zg + 0.044715 * zg * zg * zg)))
        act_ref[:, :, cols] = (act * zv).astype(BF16)
    o = jnp.dot(act_ref[...].reshape(nb * tm, -1), dn_ref[...], preferred_element_type=F32)
    o = o * lax.rsqrt(jnp.mean(o * o, axis=-1, keepdims=True) + EPS) * gpost_ref[...]
    for bi in range(nb):
        o_ref[bi] = h_ref[bi] + mod_ref[bi, 0, 5:6, :] * o[bi * tm:(bi + 1) * tm]


def _ffn(h, modtab, p, n_rows, n_lat_tiles):
    b, l, d = h.shape
    tm = ROW_TILE
    hb = tm // HALO
    n_tiles = l // tm
    d_ff = p["dn"].shape[0]
    nb = ROW_BATCH
    row = lambda bb, i: (bb, i, 0)
    consts = [p["gpre"], p["upg"], p["upv"], p["cw"], p["dn"], p["gpost"]]
    return pl.pallas_call(
        functools.partial(_ffn_kernel, n_lat_tiles=n_lat_tiles, n_tiles=n_tiles),
        grid=(b // nb, n_rows // tm),
        in_specs=[pl.BlockSpec((nb, tm, d), row),
                  pl.BlockSpec((nb, HALO, d), lambda bb, i: (bb, jnp.maximum(i * hb - 1, 0), 0)),
                  pl.BlockSpec((nb, HALO, d), lambda bb, i: (bb, jnp.minimum((i + 1) * hb, l // HALO - 1), 0)),
                  pl.BlockSpec((nb, 1, 6, d), lambda bb, i: (bb, jnp.where(i < n_lat_tiles, 0, 1), 0, 0))]
        + [_const_spec(x.shape) for x in consts],
        out_specs=pl.BlockSpec((nb, tm, d), row),
        out_shape=jax.ShapeDtypeStruct((b, n_rows, d), F32),
        scratch_shapes=[pltpu.VMEM((nb, tm + 2 * HALO, d_ff), F32), pltpu.VMEM((nb, tm, d_ff), BF16)],
        compiler_params=_cparams(2),
        name="conv_ffn",
    )(h, h, h, modtab, *consts)


def _rope_tables(n_lat, n_ctx):
    t = jnp.arange(n_lat)
    row_id = (t // GRID_W).astype(F32)
    col_id = (t % GRID_W).astype(F32)
    inv = ROPE_BASE ** (-jnp.arange(ROPE_FREQS, dtype=F32) / ROPE_FREQS)
    d = np.arange(ATT_HEAD)
    freq = d % ROPE_FREQS
    ang = jnp.where((d // (2 * ROPE_FREQS) == 0)[None, :], row_id[:, None], col_id[:, None]) * inv[freq][None, :]
    sign = np.where((d // ROPE_FREQS) % 2 == 0, -1.0, 1.0).astype(np.float32)
    cos = jnp.concatenate([jnp.cos(ang), jnp.ones((n_ctx, ATT_HEAD), F32)], axis=0)
    sin = jnp.concatenate([jnp.sin(ang) * sign[None, :], jnp.zeros((n_ctx, ATT_HEAD), F32)], axis=0)
    return jnp.tile(cos, (1, ATT_HEADS)), jnp.tile(sin, (1, ATT_HEADS))


def _dft_mats(n):
    idx = jnp.arange(n, dtype=jnp.int32)
    ang = ((idx[:, None] * idx[None, :]) % n).astype(F32) * (2.0 * math.pi / n)
    scale = 1.0 / math.sqrt(n)
    return jnp.cos(ang) * scale, jnp.sin(ang) * scale


def _block_diag(m, groups):
    return jnp.kron(jnp.eye(groups, dtype=m.dtype), m)


def kernel(x, c, ctx, c_ctx, mod_w, mod_b, norm_mix_pre, norm_mix_post, norm_ffn_pre, norm_ffn_post, w_in, rwkv_mu, rwkv_w0, rwkv_w2, rwkv_a0, rwkv_a2, rwkv_g2, rwkv_k_k, rwkv_k_a, rwkv_r_k, rwkv_lnx_w, rwkv_lnx_b, rwkv_v0, rwkv_v1, rwkv_v2, attn_sink, w_branch_rwkv, w_branch_attn, w_branch_fourier, w_out, ffn_up, ffn_conv_w, ffn_conv_b, ffn_down):
    b, n_lat, d = x.shape
    n_ctx = ctx.shape[1]
    depth = mod_w.shape[0]
    d_ff = ffn_down.shape[1]
    l = n_lat + n_ctx
    tm = ROW_TILE
    assert n_lat % tm == 0 and n_ctx % tm == 0 and n_lat % n_ctx == 0
    assert n_lat >= 3 * ATT_BLOCK and d_ff % FF_CHUNK == 0 and b % WKV_BATCH == 0 and b % ROW_BATCH == 0
    n_lat_tiles = n_lat // tm

    cos_t, sin_t = _rope_tables(n_lat, n_ctx)
    cg, sg = _dft_mats(FOURIER_GROUP_DIM)
    dft_c = jnp.concatenate([_block_diag(cg, FOURIER_GROUPS), _block_diag(sg, FOURIER_GROUPS)], axis=1).astype(BF16)
    ct_lat, st_lat = (m.astype(BF16) for m in _dft_mats(n_lat))
    ct_ctx, st_ctx = (m.astype(BF16) for m in _dft_mats(n_ctx))
    seg = _block_diag(jnp.ones((RWKV_HEAD, RWKV_HEAD), F32), RWKV_HEADS).astype(BF16)
    avg = (seg.astype(F32) / RWKV_HEAD).astype(BF16)

    pad = (-(b + 1)) % 8
    cvec = jnp.concatenate([c, c_ctx[None, :], jnp.zeros((pad, d), F32)], axis=0)
    mod = _modulation(cvec, mod_w, mod_b)

    h = jnp.concatenate([x, ctx], axis=1)
    v_first = None
    for layer in range(depth):
        last = layer == depth - 1
        lat = mod[layer, :b].reshape(b, 1, 6, d)
        cm = jnp.broadcast_to(mod[layer, b].reshape(1, 1, 6, d), (b, 1, 6, d))
        modtab = jnp.concatenate([lat, cm], axis=1)

        fp = {
            "mu": rwkv_mu[layer],
            "vec": jnp.stack([rwkv_k_k[layer], rwkv_k_a[layer], rwkv_r_k[layer]]),
            "w0": rwkv_w0[layer], "a0": rwkv_a0[layer],
            "w2": rwkv_w2[layer].astype(BF16), "a2": rwkv_a2[layer].astype(BF16),
            "g2": rwkv_g2[layer].astype(BF16), "seg": seg,
        }
        if layer > 0:
            lp = 128 - MV_LORA
            fp["v0"] = rwkv_v0[layer - 1].reshape(1, RWKV_DIM)
            fp["v1"] = jnp.pad(rwkv_v1[layer - 1], ((0, 0), (0, lp))).astype(BF16)
            fp["v2"] = jnp.pad(rwkv_v2[layer - 1], ((0, lp), (0, 0))).astype(BF16)
        q, kv, z, u_g, fc, ff, fb, ro = _inproj(h, modtab, norm_mix_pre[layer], w_in[layer], dft_c, cos_t, sin_t,
                                                v_first, fp, n_lat_tiles)
        if layer == 0:
            v_first = fc
        y_fwd, y_bwd = _wkv_scan(fc, ff, fb, n_lat)

        n_rows = n_lat if last else l
        y_att = _attention(q, kv, attn_sink[layer], n_lat, n_rows)
        y_dft = _token_dft(z, ct_lat, st_lat, n_lat, 0)
        if not last:
            y_dft = jnp.concatenate([y_dft, _token_dft(z, ct_ctx, st_ctx, n_ctx, n_lat // n_ctx)], axis=1)

        mp = {
            "ln": jnp.stack([rwkv_lnx_w[layer], rwkv_lnx_b[layer]]), "avg": avg,
            "wbr": w_branch_rwkv[layer].astype(BF16), "wba": w_branch_attn[layer].astype(BF16),
            "wbf": w_branch_fourier[layer].astype(BF16), "wo": w_out[layer].astype(BF16),
            "gpost": norm_mix_post[layer].reshape(1, d),
        }
        h = _merge(h, modtab, y_fwd, y_bwd, ro, y_att, y_dft, u_g, mp, n_rows, n_lat_tiles)

        up = ffn_up[layer].astype(BF16)
        pp = {
            "gpre": norm_ffn_pre[layer].reshape(1, d),
            "upg": up[:, :d_ff], "upv": up[:, d_ff:],
            "cw": jnp.concatenate([ffn_conv_w[layer], ffn_conv_b[layer][None, :]], axis=0),
            "dn": ffn_down[layer].astype(BF16),
            "gpost": norm_ffn_post[layer].reshape(1, d),
        }
        h = _ffn(h, modtab, pp, n_rows, n_lat_tiles)
    return h
```

```python
import functools
import math

import numpy as np
import jax
import jax.numpy as jnp
from jax import lax
from jax.experimental import pallas as pl
from jax.experimental.pallas import tpu as pltpu

F32 = jnp.float32
BF16 = jnp.bfloat16

GRID_W = 64
RWKV_HEADS = 4
RWKV_HEAD = 64
RWKV_DIM = RWKV_HEADS * RWKV_HEAD
DECAY_LORA = 64
AAA_LORA = 64
MV_LORA = 32
GATE_LORA = 128
LNX_EPS = 64e-5
ATT_HEADS = 8
ATT_KV_HEADS = 2
ATT_GROUP = ATT_HEADS // ATT_KV_HEADS
ATT_HEAD = 64
ATT_Q_DIM = ATT_HEADS * ATT_HEAD
ATT_KV_DIM = ATT_KV_HEADS * ATT_HEAD
ATT_SCALE = ATT_HEAD ** -0.5
WINDOW = 128
ROPE_BASE = 10000.0
ROPE_FREQS = ATT_HEAD // 4
NEG_INF = -1e30
FOURIER_GROUPS = 4
FOURIER_GROUP_DIM = 64
FOURIER_DIM = FOURIER_GROUPS * FOURIER_GROUP_DIM
EPS = 1e-6
RWKV_COLS = 3 * RWKV_DIM + 2 * DECAY_LORA + 2 * AAA_LORA + GATE_LORA

ROW_TILE = 256
ROW_BATCH = 2
HALO = 8
WKV_CHUNK = 64
WKV_BATCH = 4
ATT_BLOCK = 128
FF_CHUNK = 256
VMEM_LIMIT = 56 * 1024 * 1024


def _cparams(n_axes):
    return pltpu.CompilerParams(dimension_semantics=("arbitrary",) * n_axes,
                                vmem_limit_bytes=VMEM_LIMIT)


def _dot(a, b):
    return jnp.dot(a.astype(BF16), b.astype(BF16), preferred_element_type=F32)


def _dot_nt(a, b):
    return lax.dot_general(a.astype(BF16), b.astype(BF16), (((1,), (1,)), ((), ())),
                           preferred_element_type=F32)


def _split3(x):
    hi = x.astype(BF16)
    r1 = x - hi.astype(F32)
    mid = r1.astype(BF16)
    lo = (r1 - mid.astype(F32)).astype(BF16)
    return hi, mid, lo


def _dot_exact_rhs(x, m):
    hi, mid, lo = _split3(x)
    return (jnp.dot(hi, m, preferred_element_type=F32) + jnp.dot(mid, m, preferred_element_type=F32)
            + jnp.dot(lo, m, preferred_element_type=F32))


def _dot_exact_lhs(m, x):
    hi, mid, lo = _split3(x)
    return (jnp.dot(m, hi, preferred_element_type=F32) + jnp.dot(m, mid, preferred_element_type=F32)
            + jnp.dot(m, lo, preferred_element_type=F32))


def _sigmoid(x):
    return 1.0 / (1.0 + jnp.exp(-x))


def _const_spec(shape):
    nd = len(shape)
    return pl.BlockSpec(shape, lambda *_: (0,) * nd, pipeline_mode=pl.Buffered(1))


def _tile_with_halo_specs(nb, tm, d, n_own, first):
    hb = tm // HALO
    own = lambda i: jnp.clip(i - first, 0, n_own - 1)
    return [pl.BlockSpec((nb, tm, d), lambda bb, i: (bb, own(i), 0)),
            pl.BlockSpec((nb, HALO, d), lambda bb, i: (bb, jnp.maximum(own(i) * hb - 1, 0), 0)),
            pl.BlockSpec((nb, HALO, d), lambda bb, i: (bb, jnp.minimum((own(i) + 1) * hb, n_own * hb - 1), 0))]


def _rms_mod(x, g, shift, scale):
    y = x * lax.rsqrt(jnp.mean(x * x, axis=-1, keepdims=True) + EPS) * g
    return y * (1.0 + scale) + shift


def _mod_kernel(c_ref, w_ref, b_ref, o_ref):
    x = c_ref[...]
    o_ref[0] = _dot(x * _sigmoid(x), w_ref[0]) + b_ref[0]


def _modulation(cvec, mod_w, mod_b):
    depth, d, n = mod_w.shape
    rows = cvec.shape[0]
    tn = 1536
    return pl.pallas_call(
        _mod_kernel,
        grid=(depth, n // tn),
        in_specs=[pl.BlockSpec((rows, d), lambda l, j: (0, 0)),
                  pl.BlockSpec((1, d, tn), lambda l, j: (l, 0, j)),
                  pl.BlockSpec((1, 1, tn), lambda l, j: (l, 0, j))],
        out_specs=pl.BlockSpec((1, rows, tn), lambda l, j: (l, 0, j)),
        out_shape=jax.ShapeDtypeStruct((depth, rows, n), F32),
        compiler_params=_cparams(2),
        name="adaln_modulation",
    )(cvec, mod_w.astype(BF16), mod_b.reshape(depth, 1, n))


def _rope(x, cos, sin_signed):
    n = x.shape[1]
    lane = lax.broadcasted_iota(jnp.int32, x.shape, 1)
    first = (lane & ROPE_FREQS) == 0
    partner = jnp.where(first, pltpu.roll(x, n - ROPE_FREQS, 1), pltpu.roll(x, ROPE_FREQS, 1))
    return x * cos + partner * sin_signed


def _inproj_kernel(*refs, split_input, has_vres, n_lat_tiles, n_tiles):
    i = pl.program_id(1)
    if split_input:
        is_lat = i < n_lat_tiles
        lat_refs, ctx_refs = refs[0:3], refs[3:6]
        tile_of = lambda bi, k: jnp.where(is_lat, lat_refs[k][bi], ctx_refs[k][bi])
        refs = refs[3:]
    else:
        lat_refs = refs[0:3]
        tile_of = lambda bi, k: lat_refs[k][bi]
    h_ref = refs[0]
    (mod_ref, g_ref, wr_ref, wq_ref, wkv_ref, wf_ref, wg_ref, dft_ref, cos_ref, sin_ref,
     mu_ref, vec_ref, w0_ref, a0_ref, w2_ref, a2_ref, g2_ref, seg_ref) = refs[3:21]
    refs = refs[21:]
    if has_vres:
        vf_ref, v0_ref, v1_ref, v2_ref = refs[:4]
        refs = refs[4:]
    q_ref, kv_ref, z_ref, ug_ref, fc_ref, ff_ref, fb_ref, ro_ref, buf_ref = refs
    nb, tm, _ = h_ref.shape
    th = tm + 2 * HALO
    prev_ok = jnp.logical_and(i != 0, i != n_lat_tiles)
    next_ok = jnp.logical_and(i != n_lat_tiles - 1, i != n_tiles - 1)
    rid = lax.broadcasted_iota(jnp.int32, (th, 1), 0)
    live = jnp.logical_and(jnp.logical_or(rid >= HALO, prev_ok), jnp.logical_or(rid < HALO + tm, next_ok))
    a_all, a_main = [], []
    for bi in range(nb):
        x = jnp.concatenate([tile_of(bi, 1), tile_of(bi, 0), tile_of(bi, 2)], axis=0)
        a = _rms_mod(x, g_ref[...], mod_ref[bi, 0, 0:1, :], mod_ref[bi, 0, 1:2, :])
        a_all.append(jnp.where(live, a, 0.0).astype(BF16))
        a_main.append(a[HALO:HALO + tm].astype(BF16))
    a_all = jnp.concatenate(a_all, axis=0)
    a = jnp.concatenate(a_main, axis=0)
    cos = jnp.concatenate([cos_ref[...]] * nb, axis=0)
    sin = jnp.concatenate([sin_ref[...]] * nb, axis=0)
    buf_ref[...] = jnp.dot(a_all, wr_ref[...], preferred_element_type=F32).reshape(nb, th, -1)
    q = jnp.dot(a, wq_ref[...], preferred_element_type=F32)
    q_ref[...] = (_rope(q, cos, sin) * ATT_SCALE).astype(BF16).reshape(nb, tm, -1)
    kv = jnp.dot(a, wkv_ref[...], preferred_element_type=F32)
    k = _rope(kv[:, :ATT_KV_DIM], cos[:, :ATT_KV_DIM], sin[:, :ATT_KV_DIM])
    kv_ref[:, :, :ATT_KV_DIM] = k.astype(BF16).reshape(nb, tm, -1)
    kv_ref[:, :, ATT_KV_DIM:] = kv[:, ATT_KV_DIM:].astype(BF16).reshape(nb, tm, -1)
    uf = jnp.dot(a, wf_ref[...], preferred_element_type=F32)
    z_ref[...] = _dot(uf, dft_ref[...]).astype(BF16).reshape(nb, tm, -1)
    half = wg_ref.shape[1] // 2
    ug_ref[:, :, :half] = jnp.dot(a, wg_ref[:, :half], preferred_element_type=F32).astype(BF16).reshape(nb, tm, -1)
    _rwkv_features(buf_ref, tm, vf_ref if has_vres else None, (v0_ref, v1_ref, v2_ref) if has_vres else None,
                   mu_ref, vec_ref, w0_ref, a0_ref, w2_ref, a2_ref, g2_ref, seg_ref, fc_ref, ff_ref, fb_ref, ro_ref)
    ug_ref[:, :, half:] = jnp.dot(a, wg_ref[:, half:], preferred_element_type=F32).astype(BF16).reshape(nb, tm, -1)


def _inproj(h, modtab, g, w_in, dft_c, cos_t, sin_t, v_first_src, p, n_lat_tiles):
    split_input = isinstance(h, tuple)
    parts = h if split_input else (h,)
    b, _, d = parts[0].shape
    l = sum(x.shape[1] for x in parts)
    tm = ROW_TILE
    n = RWKV_DIM
    n_tiles = l // tm
    hb = tm // HALO
    n_g = w_in.shape[1] - (RWKV_COLS + ATT_Q_DIM + 2 * ATT_KV_DIM + FOURIER_DIM)
    o = np.cumsum([0, RWKV_COLS, ATT_Q_DIM, 2 * ATT_KV_DIM, FOURIER_DIM, n_g])
    wb = w_in.astype(BF16)
    ws = [wb[:, o[i]:o[i + 1]] for i in range(5)]
    nb = ROW_BATCH
    has_vres = v_first_src is not None
    row = lambda bb, i: (bb, i, 0)
    consts = [g.reshape(1, d)] + ws + [dft_c]
    feat_consts = [p["mu"], p["vec"], p["w0"], p["a0"], p["w2"], p["a2"], p["g2"], p["seg"]]
    in_specs, args, first = [], [], 0
    for x in parts:
        in_specs += _tile_with_halo_specs(nb, tm, d, x.shape[1] // tm, first)
        args += [x, x, x]
        first += x.shape[1] // tm
    in_specs += ([pl.BlockSpec((nb, 1, 6, d), lambda bb, i: (bb, jnp.where(i < n_lat_tiles, 0, 1), 0, 0))]
                 + [_const_spec(x.shape) for x in consts]
                 + [pl.BlockSpec((tm, ATT_Q_DIM), lambda bb, i: (i, 0)),
                    pl.BlockSpec((tm, ATT_Q_DIM), lambda bb, i: (i, 0))]
                 + [_const_spec(x.shape) for x in feat_consts])
    args += [modtab] + consts + [cos_t, sin_t] + feat_consts
    if has_vres:
        vres_consts = [p["v0"], p["v1"], p["v2"]]
        in_specs += [pl.BlockSpec((nb, tm, n), lambda bb, i: (bb, i, 1))] + [_const_spec(x.shape) for x in vres_consts]
        args += [v_first_src] + vres_consts
    outs = [(ATT_Q_DIM, BF16), (2 * ATT_KV_DIM, BF16), (2 * FOURIER_DIM, BF16), (n_g, BF16),
            (3 * n, F32), (3 * n, F32), (3 * n, F32), (2 * n, F32)]
    return pl.pallas_call(
        functools.partial(_inproj_kernel, split_input=split_input, has_vres=has_vres, n_lat_tiles=n_lat_tiles,
                          n_tiles=n_tiles),
        grid=(b // nb, n_tiles),
        in_specs=in_specs,
        out_specs=[pl.BlockSpec((nb, tm, w), row) for w, _ in outs],
        out_shape=[jax.ShapeDtypeStruct((b, l, w), dt) for w, dt in outs],
        scratch_shapes=[pltpu.VMEM((nb, tm + 2 * HALO, RWKV_COLS), F32)],
        compiler_params=_cparams(2),
        name="in_projection",
    )(*args)


def _rwkv_features(buf_ref, tm, vf_ref, vres, mu_ref, vec_ref, w0_ref, a0_ref, w2_ref, a2_ref, g2_ref, seg_ref,
                   fc_ref, ff_ref, fb_ref, ro_ref):
    n = RWKV_DIM
    k_k = vec_ref[0:1, :]
    k_a = vec_ref[1:2, :]
    r_k = vec_ref[2:3, :]
    seg = seg_ref[...]
    tiles = []
    for bi in range(buf_ref.shape[0]):
        u = buf_ref[bi, HALO:HALO + tm, :]
        u_prev = buf_ref[bi, HALO - 1:HALO - 1 + tm, :]
        u_next = buf_ref[bi, HALO + 1:HALO + 1 + tm, :]
        us = u + mu_ref[0:1, :] * (u_prev - u) + mu_ref[1:2, :] * (u_next - u)
        o = 3 * n
        t = dict(bi=bi, r=us[:, 0:n], k=us[:, n:2 * n], v=us[:, 2 * n:3 * n])
        t["tw"] = [jnp.tanh(us[:, o + d * DECAY_LORA:o + (d + 1) * DECAY_LORA]).astype(BF16) for d in range(2)]
        o += 2 * DECAY_LORA
        t["xa"] = [us[:, o + d * AAA_LORA:o + (d + 1) * AAA_LORA].astype(BF16) for d in range(2)]
        o += 2 * AAA_LORA
        t["sg"] = _sigmoid(us[:, o:o + GATE_LORA]).astype(BF16)
        t["kk"] = t["k"] * k_k
        t["kk_sq"] = _split3(t["kk"] * t["kk"])
        tiles.append(t)
    for t in tiles:
        t["z"] = [jnp.dot(t["tw"][d], w2_ref[d], preferred_element_type=F32) for d in range(2)]
        t["za"] = [jnp.dot(t["xa"][d], a2_ref[d], preferred_element_type=F32) for d in range(2)]
        t["g"] = jnp.dot(t["sg"], g2_ref[...], preferred_element_type=F32)
        t["ss"] = sum(jnp.dot(part, seg, preferred_element_type=F32) for part in t["kk_sq"])
        if vres is not None:
            t["vv"] = _dot(t["v"], vres[1][...])
    if vres is not None:
        for t in tiles:
            t["vg"] = _dot(t["vv"], vres[2][...])
    for t in tiles:
        bi = t["bi"]
        v = t["v"]
        if vres is not None:
            v = v + (vf_ref[bi] - v) * _sigmoid(vres[0][...] + t["vg"])
        kk = t["kk"] * lax.rsqrt(jnp.maximum(t["ss"], 1e-24))
        keys = []
        for d, out_ref in enumerate((ff_ref, fb_ref)):
            z = w0_ref[d:d + 1, :] + t["z"][d]
            softplus = jnp.maximum(-z, 0.0) + jnp.log(1.0 + jnp.exp(-jnp.abs(z)))
            out_ref[bi, :, 0:n] = -jnp.exp(-softplus - 0.5)
            a = _sigmoid(a0_ref[d:d + 1, :] + t["za"][d])
            key = t["k"] * (1.0 + (a - 1.0) * k_a)
            keys.append(key)
            out_ref[bi, :, n:2 * n] = key
            out_ref[bi, :, 2 * n:3 * n] = kk * a
        t["v"] = v
        t["bonus_in"] = _split3(t["r"] * (0.5 * (keys[0] + keys[1])) * r_k)
        fc_ref[bi, :, 0:n] = t["r"]
        fc_ref[bi, :, n:2 * n] = v
        fc_ref[bi, :, 2 * n:3 * n] = kk
        ro_ref[bi, :, n:2 * n] = t["g"]
    for t in tiles:
        bonus = sum(jnp.dot(part, seg, preferred_element_type=F32) for part in t["bonus_in"])
        ro_ref[t["bi"], :, 0:n] = bonus * t["v"]


def _wkv_masks(c, reverse):
    ti = lax.broadcasted_iota(jnp.int32, (c, RWKV_HEADS * c), 0)
    si = lax.broadcasted_iota(jnp.int32, (c, RWKV_HEADS * c), 1) & (c - 1)
    incl, strict = (si >= ti, si > ti) if reverse else (si <= ti, si < ti)
    levels = []
    for sh in range(int(math.log2(c))):
        bt = lax.shift_right_logical(ti, sh)
        bs = lax.shift_right_logical(si, sh)
        if reverse:
            levels.append(jnp.logical_and((bt & 1) == 0, bs == bt + 1))
        else:
            levels.append(jnp.logical_and((bt & 1) == 1, bs == bt - 1))
    return incl, strict, levels, jnp.where(ti == si, 1.0, 0.0)


def _cumsum_rows(x, reverse):
    rows = x.shape[0]
    t = lax.broadcasted_iota(jnp.int32, x.shape, 0)
    s = 1
    while s < rows:
        if reverse:
            x = x + jnp.where(t < rows - s, pltpu.roll(x, rows - s, 0), 0.0)
        else:
            x = x + jnp.where(t >= s, pltpu.roll(x, s, 0), 0.0)
        s *= 2
    return x


def _head_blocks(x, same_head):
    return jnp.where(same_head[:RWKV_HEADS * x.shape[0]], jnp.concatenate([x] * RWKV_HEADS, axis=0), 0.0).astype(BF16)


def _wkv_kernel(fcf_ref, ff_ref, fcb_ref, fb_ref, yf_ref, yb_ref, sf_ref, sb_ref):
    @pl.when(pl.program_id(1) == 0)
    def _():
        sf_ref[...] = jnp.zeros_like(sf_ref)
        sb_ref[...] = jnp.zeros_like(sb_ref)

    c = WKV_CHUNK
    n = RWKV_DIM
    hd = RWKV_HEAD
    nb = fcf_ref.shape[0]
    assert c == hd
    sh = int(math.log2(c))
    same_c = (lax.shift_right_logical(lax.broadcasted_iota(jnp.int32, (RWKV_HEADS * c, n), 0), sh)
              == lax.shift_right_logical(lax.broadcasted_iota(jnp.int32, (RWKV_HEADS * c, n), 1), sh))
    groups = []
    for fc_ref, fd_ref, y_ref, s_ref, reverse in ((fcf_ref, ff_ref, yf_ref, sf_ref, False),
                                                  (fcb_ref, fb_ref, yb_ref, sb_ref, True)):
        incl, strict, levels, eye = _wkv_masks(c, reverse)
        for bi in range(nb):
            cum = _cumsum_rows(fd_ref[bi, :, 0:n], reverse)
            total = cum[0:1, :] if reverse else cum[c - 1:c, :]
            c0 = 0.5 * total
            e_neg = jnp.exp(c0 - cum)
            a_t = -fc_ref[bi, :, 2 * n:3 * n] * jnp.exp(cum - fd_ref[bi, :, 0:n] - c0)
            r_t = fc_ref[bi, :, 0:n] * jnp.exp(cum - c0)
            e_half = jnp.exp(c0)
            groups.append(dict(
                lhs=jnp.concatenate([a_t, r_t], axis=0).astype(BF16),
                b_t=fd_ref[bi, :, 2 * n:3 * n] * e_neg, k_t=fd_ref[bi, :, n:2 * n] * e_neg,
                v=fc_ref[bi, :, n:2 * n], s0=s_ref[bi], e_half=e_half, e_tot=jnp.exp(total),
                incl=incl, strict=strict, levels=levels, eye=eye, y_ref=y_ref, s_ref=s_ref, bi=bi))

    for g in groups:
        rhs = jnp.concatenate([_head_blocks(g["b_t"], same_c), _head_blocks(g["k_t"], same_c),
                               _head_blocks(g["s0"] * g["e_half"], same_c)], axis=0)
        prod = _dot_nt(g["lhs"], rhs)
        g["gb"] = prod[:, 0:n]
        g["gk"] = prod[:, n:2 * n]
        g["a_s"] = prod[:, 2 * n:3 * n]
    for g in groups:
        g["a_ab"] = jnp.where(g["strict"], g["gb"][:c], 0.0)
        a_ak = jnp.where(g["strict"], g["gk"][:c], 0.0)
        g["a_r"] = jnp.concatenate([jnp.where(g["incl"], g["gb"][c:], 0.0),
                                    jnp.where(g["incl"], g["gk"][c:], 0.0)], axis=1).astype(BF16)
        g["t"] = g["eye"] + jnp.where(g["levels"][0], g["a_ab"], 0.0)
        g["v_blocks"] = _head_blocks(g["v"], same_c)
        g["rhs_u"] = g["a_s"][:c] + jnp.dot(a_ak.astype(BF16), g["v_blocks"], preferred_element_type=F32)
    for lvl in range(1, len(groups[0]["levels"])):
        for g in groups:
            g["tb"] = g["t"].astype(BF16)
            g["tmp"] = jnp.dot(jnp.where(g["levels"][lvl], g["a_ab"], 0.0).astype(BF16),
                               _head_blocks(g["t"], same_c), preferred_element_type=F32)
        for g in groups:
            g["t"] = g["t"] + jnp.dot(g["tb"], _head_blocks(g["tmp"], same_c), preferred_element_type=F32)
    for g in groups:
        g["u"] = jnp.dot(g["t"].astype(BF16), _head_blocks(g["rhs_u"], same_c), preferred_element_type=F32)
    for g in groups:
        uv_blocks = jnp.concatenate([_head_blocks(g["u"], same_c), g["v_blocks"]], axis=0)
        g["y_ref"][g["bi"]] = g["a_s"][c:] + jnp.dot(g["a_r"], uv_blocks, preferred_element_type=F32)
    for g in groups:
        uv = jnp.concatenate([g["u"], g["v"]], axis=0).astype(BF16)
        bk = jnp.concatenate([g["b_t"], g["k_t"]], axis=0).astype(BF16)
        upd = lax.dot_general(uv, bk, (((0,), (0,)), ((), ())), preferred_element_type=F32)
        upd = jnp.where(same_c[:n], upd, 0.0)
        upd = sum(upd[h * hd:(h + 1) * hd] for h in range(RWKV_HEADS))
        g["s_ref"][g["bi"]] = g["s0"] * g["e_tot"] + upd * g["e_half"]


def _wkv_scan(fc, ff, fb, n_lat):
    b, l, w = fc.shape
    c = WKV_CHUNK
    nb = WKV_BATCH
    nl = n_lat // c
    nc = l // c - nl
    fwd = lambda bb, j: (bb, jnp.where(j < nc, nl + j, j - nc), 0)
    bwd = lambda bb, j: (bb, nl + nc - 1 - j, 0)
    return pl.pallas_call(
        _wkv_kernel,
        grid=(b // nb, nl + nc),
        in_specs=[pl.BlockSpec((nb, c, w), fwd), pl.BlockSpec((nb, c, w), fwd),
                  pl.BlockSpec((nb, c, w), bwd), pl.BlockSpec((nb, c, w), bwd)],
        out_specs=[pl.BlockSpec((nb, c, RWKV_DIM), fwd), pl.BlockSpec((nb, c, RWKV_DIM), bwd)],
        out_shape=[jax.ShapeDtypeStruct((b, l, RWKV_DIM), F32)] * 2,
        scratch_shapes=[pltpu.VMEM((nb, RWKV_HEAD, RWKV_DIM), F32)] * 2,
        compiler_params=_cparams(2),
        name="wkv_scan",
    )(fc, ff, fc, fb)


def _attn_kernel(sink_ref, q_ref, kv_ref, o_ref, *, n_lat):
    j = pl.program_id(1)
    qb = ATT_BLOCK
    hd = ATT_HEAD
    n_ctx = kv_ref.shape[1] - n_lat
    n_win = 3 * qb
    is_lat = j * qb < n_lat
    ws = pl.multiple_of(jnp.clip((j - 1) * qb, 0, n_lat - n_win), qb)
    col = lax.broadcasted_iota(jnp.int32, (qb, n_ctx + n_win), 1)
    q_pos = j * qb + lax.broadcasted_iota(jnp.int32, (qb, n_ctx + n_win), 0)
    k_pos = ws + col - n_ctx
    valid = jnp.logical_or(col < n_ctx, jnp.logical_and(jnp.abs(q_pos - k_pos) <= WINDOW, is_lat))
    q = q_ref[0]
    kv = jnp.concatenate([kv_ref[0, n_lat:n_lat + n_ctx, :], kv_ref[0, pl.ds(ws, n_win), :]], axis=0)
    heads = range(ATT_HEADS)
    s = [jnp.where(valid, _dot_nt(q[:, h * hd:(h + 1) * hd],
                                  kv[:, (h // ATT_GROUP) * hd:(h // ATT_GROUP + 1) * hd]), NEG_INF) for h in heads]
    m = [jnp.maximum(jnp.max(s[h], axis=-1, keepdims=True), sink_ref[h]) for h in heads]
    p = [jnp.exp(s[h] - m[h]) for h in heads]
    den = [jnp.sum(p[h], axis=-1, keepdims=True) + jnp.exp(sink_ref[h] - m[h]) for h in heads]
    o = [_dot(p[h], kv[:, ATT_KV_DIM + (h // ATT_GROUP) * hd:ATT_KV_DIM + (h // ATT_GROUP + 1) * hd]) for h in heads]
    for h in heads:
        o_ref[0, :, h * hd:(h + 1) * hd] = (o[h] / den[h]).astype(o_ref.dtype)


def _attention(q, kv, sink, n_lat, n_rows):
    b, l, _ = q.shape
    qb = ATT_BLOCK
    return pl.pallas_call(
        functools.partial(_attn_kernel, n_lat=n_lat),
        grid=(b, n_rows // qb),
        in_specs=[pl.BlockSpec(memory_space=pltpu.SMEM),
                  pl.BlockSpec((1, qb, ATT_Q_DIM), lambda bb, j: (bb, j, 0)),
                  pl.BlockSpec((1, l, kv.shape[2]), lambda bb, j: (bb, 0, 0))],
        out_specs=pl.BlockSpec((1, qb, ATT_Q_DIM), lambda bb, j: (bb, j, 0)),
        out_shape=jax.ShapeDtypeStruct((b, n_rows, ATT_Q_DIM), BF16),
        compiler_params=_cparams(2),
        name="windowed_attention",
    )(sink, q, kv)


def _dft_kernel(z_ref, ct_ref, st_ref, o_ref):
    n = FOURIER_DIM
    o_ref[0] = (jnp.dot(ct_ref[...], z_ref[0, :, 0:n], preferred_element_type=F32)
                - jnp.dot(st_ref[...], z_ref[0, :, n:2 * n], preferred_element_type=F32)).astype(o_ref.dtype)


def _token_dft(z, ct, st, seg_rows, seg_block):
    b = z.shape[0]
    return pl.pallas_call(
        _dft_kernel,
        grid=(b,),
        in_specs=[pl.BlockSpec((1, seg_rows, z.shape[2]), lambda bb: (bb, seg_block, 0)),
                  _const_spec(ct.shape), _const_spec(st.shape)],
        out_specs=pl.BlockSpec((1, seg_rows, FOURIER_DIM), lambda bb: (bb, 0, 0)),
        out_shape=jax.ShapeDtypeStruct((b, seg_rows, FOURIER_DIM), BF16),
        compiler_params=_cparams(1),
        name="token_dft",
    )(z, ct, st)


def _merge_kernel(*refs, split_input, n_lat_tiles):
    if split_input:
        is_lat = pl.program_id(1) < n_lat_tiles
        x_ref, c_ref = refs[0:2]
        residual = lambda bi: jnp.where(is_lat, x_ref[bi], c_ref[bi])
        refs = refs[1:]
    else:
        residual = lambda bi: h_ref[bi]
    (h_ref, mod_ref, yf_ref, yb_ref, ro_ref, ya_ref, yd_ref, ug_ref, ln_ref, avg_ref,
     wbr_ref, wba_ref, wbf_ref, wo_ref, gp_ref, o_ref) = refs
    n = RWKV_DIM
    d = h_ref.shape[2]
    avg = avg_ref[...]
    nb, tm, _ = h_ref.shape
    rows = nb * tm
    y = (yf_ref[...] + yb_ref[...]).reshape(rows, n)
    dev = y - _dot_exact_rhs(y, avg)
    var = _dot_exact_rhs(dev * dev, avg)
    yn = dev * lax.rsqrt(var + LNX_EPS) * ln_ref[0:1, :] + ln_ref[1:2, :]
    ro = ro_ref[...].reshape(rows, 2 * n)
    y_r = (yn + ro[:, 0:n]) * ro[:, n:2 * n]
    mix = _sigmoid(ug_ref[:, :, 0:d].reshape(rows, d).astype(F32)) * _dot(y_r, wbr_ref[...])
    mix += (_sigmoid(ug_ref[:, :, d:2 * d].reshape(rows, d).astype(F32))
            * jnp.dot(ya_ref[...].reshape(rows, -1), wba_ref[...], preferred_element_type=F32))
    mix += (_sigmoid(ug_ref[:, :, 2 * d:3 * d].reshape(rows, d).astype(F32))
            * jnp.dot(yd_ref[...].reshape(rows, -1), wbf_ref[...], preferred_element_type=F32))
    o = _dot(mix, wo_ref[...])
    o = o * lax.rsqrt(jnp.mean(o * o, axis=-1, keepdims=True) + EPS) * gp_ref[...]
    for bi in range(nb):
        o_ref[bi] = residual(bi) + mod_ref[bi, 0, 2:3, :] * o[bi * tm:(bi + 1) * tm]


def _merge(h, modtab, yf, yb, ro, ya, yd, ug, p, n_rows, n_lat_tiles):
    split_input = isinstance(h, tuple)
    parts = h if split_input else (h,)
    b, _, d = parts[0].shape
    tm = ROW_TILE
    nb = ROW_BATCH
    row = lambda bb, i: (bb, i, 0)
    consts = [p["ln"], p["avg"], p["wbr"], p["wba"], p["wbf"], p["wo"], p["gpost"]]
    h_specs, first = [], 0
    for x in parts:
        h_specs.append(_tile_with_halo_specs(nb, tm, d, x.shape[1] // tm, first)[0])
        first += x.shape[1] // tm
    return pl.pallas_call(
        functools.partial(_merge_kernel, split_input=split_input, n_lat_tiles=n_lat_tiles),
        grid=(b // nb, n_rows // tm),
        in_specs=h_specs + [
                  pl.BlockSpec((nb, 1, 6, d), lambda bb, i: (bb, jnp.where(i < n_lat_tiles, 0, 1), 0, 0)),
                  pl.BlockSpec((nb, tm, RWKV_DIM), row), pl.BlockSpec((nb, tm, RWKV_DIM), row),
                  pl.BlockSpec((nb, tm, 2 * RWKV_DIM), row),
                  pl.BlockSpec((nb, tm, ATT_Q_DIM), row), pl.BlockSpec((nb, tm, FOURIER_DIM), row),
                  pl.BlockSpec((nb, tm, 3 * d), row)] + [_const_spec(x.shape) for x in consts],
        out_specs=pl.BlockSpec((nb, tm, d), row),
        out_shape=jax.ShapeDtypeStruct((b, n_rows, d), F32),
        compiler_params=_cparams(2),
        name="branch_merge",
    )(*parts, modtab, yf, yb, ro, ya, yd, ug, *consts)


def _ffn_kernel(h_ref, hp_ref, hn_ref, mod_ref, gpre_ref, upg_ref, upv_ref, cw_ref, dn_ref, gpost_ref, o_ref,
                zg_ref, act_ref, *, n_lat_tiles, n_tiles):
    i = pl.program_id(1)
    tm = h_ref.shape[1]
    prev_ok = jnp.logical_and(i != 0, i != n_lat_tiles)
    next_ok = jnp.logical_and(i != n_lat_tiles - 1, i != n_tiles - 1)
    rid = lax.broadcasted_iota(jnp.int32, (tm + 2 * HALO, 1), 0)
    live = jnp.logical_and(jnp.logical_or(rid >= HALO, prev_ok), jnp.logical_or(rid < HALO + tm, next_ok))
    nb = h_ref.shape[0]
    th = tm + 2 * HALO
    f_all, f_main = [], []
    for bi in range(nb):
        x = jnp.concatenate([hp_ref[bi], h_ref[bi], hn_ref[bi]], axis=0)
        f = _rms_mod(x, gpre_ref[...], mod_ref[bi, 0, 3:4, :], mod_ref[bi, 0, 4:5, :])
        f_all.append(jnp.where(live, f, 0.0).astype(BF16))
        f_main.append(f[HALO:HALO + tm].astype(BF16))
    f_all = jnp.concatenate(f_all, axis=0)
    f_main = jnp.concatenate(f_main, axis=0)
    for c in range(zg_ref.shape[2] // FF_CHUNK):
        cols = slice(c * FF_CHUNK, (c + 1) * FF_CHUNK)
        zg_ref[:, :, cols] = jnp.dot(f_all, upg_ref[:, cols], preferred_element_type=F32).reshape(nb, th, -1)
        zv = jnp.dot(f_main, upv_ref[:, cols], preferred_element_type=F32).reshape(nb, tm, -1)
        zg = (cw_ref[0:1, cols] * zg_ref[:, HALO - 1:HALO - 1 + tm, cols]
              + cw_ref[1:2, cols] * zg_ref[:, HALO:HALO + tm, cols]
              + cw_ref[2:3, cols] * zg_ref[:, HALO + 1:HALO + 1 + tm, cols] + cw_ref[3:4, cols])
        act = 0.5 * zg * (1.0 + jnp.tanh(0.7978845608028654 * (zg + 0.044715 * zg * zg * zg)))
        act_ref[:, :, cols] = (act * zv).astype(BF16)
    o = jnp.dot(act_ref[...].reshape(nb * tm, -1), dn_ref[...], preferred_element_type=F32)
    o = o * lax.rsqrt(jnp.mean(o * o, axis=-1, keepdims=True) + EPS) * gpost_ref[...]
    for bi in range(nb):
        o_ref[bi] = h_ref[bi] + mod_ref[bi, 0, 5:6, :] * o[bi * tm:(bi + 1) * tm]


def _ffn(h, modtab, p, n_rows, n_lat_tiles):
    b, l, d = h.shape
    tm = ROW_TILE
    hb = tm // HALO
    n_tiles = l // tm
    d_ff = p["dn"].shape[0]
    nb = ROW_BATCH
    row = lambda bb, i: (bb, i, 0)
    consts = [p["gpre"], p["upg"], p["upv"], p["cw"], p["dn"], p["gpost"]]
    return pl.pallas_call(
        functools.partial(_ffn_kernel, n_lat_tiles=n_lat_tiles, n_tiles=n_tiles),
        grid=(b // nb, n_rows // tm),
        in_specs=[pl.BlockSpec((nb, tm, d), row),
                  pl.BlockSpec((nb, HALO, d), lambda bb, i: (bb, jnp.maximum(i * hb - 1, 0), 0)),
                  pl.BlockSpec((nb, HALO, d), lambda bb, i: (bb, jnp.minimum((i + 1) * hb, l // HALO - 1), 0)),
                  pl.BlockSpec((nb, 1, 6, d), lambda bb, i: (bb, jnp.where(i < n_lat_tiles, 0, 1), 0, 0))]
        + [_const_spec(x.shape) for x in consts],
        out_specs=pl.BlockSpec((nb, tm, d), row),
        out_shape=jax.ShapeDtypeStruct((b, n_rows, d), F32),
        scratch_shapes=[pltpu.VMEM((nb, tm + 2 * HALO, d_ff), F32), pltpu.VMEM((nb, tm, d_ff), BF16)],
        compiler_params=_cparams(2),
        name="conv_ffn",
    )(h, h, h, modtab, *consts)


def _rope_tables(n_lat, n_ctx):
    t = jnp.arange(n_lat)
    row_id = (t // GRID_W).astype(F32)
    col_id = (t % GRID_W).astype(F32)
    inv = ROPE_BASE ** (-jnp.arange(ROPE_FREQS, dtype=F32) / ROPE_FREQS)
    d = np.arange(ATT_HEAD)
    freq = d % ROPE_FREQS
    ang = jnp.where((d // (2 * ROPE_FREQS) == 0)[None, :], row_id[:, None], col_id[:, None]) * inv[freq][None, :]
    sign = np.where((d // ROPE_FREQS) % 2 == 0, -1.0, 1.0).astype(np.float32)
    cos = jnp.concatenate([jnp.cos(ang), jnp.ones((n_ctx, ATT_HEAD), F32)], axis=0)
    sin = jnp.concatenate([jnp.sin(ang) * sign[None, :], jnp.zeros((n_ctx, ATT_HEAD), F32)], axis=0)
    return jnp.tile(cos, (1, ATT_HEADS)), jnp.tile(sin, (1, ATT_HEADS))


def _dft_mats(n):
    idx = jnp.arange(n, dtype=jnp.int32)
    ang = ((idx[:, None] * idx[None, :]) % n).astype(F32) * (2.0 * math.pi / n)
    scale = 1.0 / math.sqrt(n)
    return jnp.cos(ang) * scale, jnp.sin(ang) * scale


def _block_diag(m, groups):
    return jnp.kron(jnp.eye(groups, dtype=m.dtype), m)


def kernel(x, c, ctx, c_ctx, mod_w, mod_b, norm_mix_pre, norm_mix_post, norm_ffn_pre, norm_ffn_post, w_in, rwkv_mu, rwkv_w0, rwkv_w2, rwkv_a0, rwkv_a2, rwkv_g2, rwkv_k_k, rwkv_k_a, rwkv_r_k, rwkv_lnx_w, rwkv_lnx_b, rwkv_v0, rwkv_v1, rwkv_v2, attn_sink, w_branch_rwkv, w_branch_attn, w_branch_fourier, w_out, ffn_up, ffn_conv_w, ffn_conv_b, ffn_down):
    b, n_lat, d = x.shape
    n_ctx = ctx.shape[1]
    depth = mod_w.shape[0]
    d_ff = ffn_down.shape[1]
    l = n_lat + n_ctx
    tm = ROW_TILE
    assert n_lat % tm == 0 and n_ctx % tm == 0 and n_lat % n_ctx == 0
    assert n_lat >= 3 * ATT_BLOCK and d_ff % FF_CHUNK == 0 and b % WKV_BATCH == 0 and b % ROW_BATCH == 0
    n_lat_tiles = n_lat // tm

    cos_t, sin_t = _rope_tables(n_lat, n_ctx)
    cg, sg = _dft_mats(FOURIER_GROUP_DIM)
    dft_c = jnp.concatenate([_block_diag(cg, FOURIER_GROUPS), _block_diag(sg, FOURIER_GROUPS)], axis=1).astype(BF16)
    ct_lat, st_lat = (m.astype(BF16) for m in _dft_mats(n_lat))
    ct_ctx, st_ctx = (m.astype(BF16) for m in _dft_mats(n_ctx))
    seg = _block_diag(jnp.ones((RWKV_HEAD, RWKV_HEAD), F32), RWKV_HEADS).astype(BF16)
    avg = (seg.astype(F32) / RWKV_HEAD).astype(BF16)

    pad = (-(b + 1)) % 8
    cvec = jnp.concatenate([c, c_ctx[None, :], jnp.zeros((pad, d), F32)], axis=0)
    mod = _modulation(cvec, mod_w, mod_b)

    h = (x, ctx)
    v_first = None
    for layer in range(depth):
        last = layer == depth - 1
        lat = mod[layer, :b].reshape(b, 1, 6, d)
        cm = jnp.broadcast_to(mod[layer, b].reshape(1, 1, 6, d), (b, 1, 6, d))
        modtab = jnp.concatenate([lat, cm], axis=1)

        fp = {
            "mu": rwkv_mu[layer],
            "vec": jnp.stack([rwkv_k_k[layer], rwkv_k_a[layer], rwkv_r_k[layer]]),
            "w0": rwkv_w0[layer], "a0": rwkv_a0[layer],
            "w2": rwkv_w2[layer].astype(BF16), "a2": rwkv_a2[layer].astype(BF16),
            "g2": rwkv_g2[layer].astype(BF16), "seg": seg,
        }
        if layer > 0:
            lp = 128 - MV_LORA
            fp["v0"] = rwkv_v0[layer - 1].reshape(1, RWKV_DIM)
            fp["v1"] = jnp.pad(rwkv_v1[layer - 1], ((0, 0), (0, lp))).astype(BF16)
            fp["v2"] = jnp.pad(rwkv_v2[layer - 1], ((0, lp), (0, 0))).astype(BF16)
        q, kv, z, u_g, fc, ff, fb, ro = _inproj(h, modtab, norm_mix_pre[layer], w_in[layer], dft_c, cos_t, sin_t,
                                                v_first, fp, n_lat_tiles)
        if layer == 0:
            v_first = fc
        y_fwd, y_bwd = _wkv_scan(fc, ff, fb, n_lat)

        n_rows = n_lat if last else l
        y_att = _attention(q, kv, attn_sink[layer], n_lat, n_rows)
        y_dft = _token_dft(z, ct_lat, st_lat, n_lat, 0)
        if not last:
            y_dft = jnp.concatenate([y_dft, _token_dft(z, ct_ctx, st_ctx, n_ctx, n_lat // n_ctx)], axis=1)

        mp = {
            "ln": jnp.stack([rwkv_lnx_w[layer], rwkv_lnx_b[layer]]), "avg": avg,
            "wbr": w_branch_rwkv[layer].astype(BF16), "wba": w_branch_attn[layer].astype(BF16),
            "wbf": w_branch_fourier[layer].astype(BF16), "wo": w_out[layer].astype(BF16),
            "gpost": norm_mix_post[layer].reshape(1, d),
        }
        h = _merge(h, modtab, y_fwd, y_bwd, ro, y_att, y_dft, u_g, mp, n_rows, n_lat_tiles)

        up = ffn_up[layer].astype(BF16)
        pp = {
            "gpre": norm_ffn_pre[layer].reshape(1, d),
            "upg": up[:, :d_ff], "upv": up[:, d_ff:],
            "cw": jnp.concatenate([ffn_conv_w[layer], ffn_conv_b[layer][None, :]], axis=0),
            "dn": ffn_down[layer].astype(BF16),
            "gpost": norm_ffn_post[layer].reshape(1, d),
        }
        h = _ffn(h, modtab, pp, n_rows, n_lat_tiles)
    return h
```

```python
import functools
import math

import numpy as np
import jax
import jax.numpy as jnp
from jax import lax
from jax.experimental import pallas as pl
from jax.experimental.pallas import tpu as pltpu

F32 = jnp.float32
BF16 = jnp.bfloat16

GRID_W = 64
RWKV_HEADS = 4
RWKV_HEAD = 64
RWKV_DIM = RWKV_HEADS * RWKV_HEAD
DECAY_LORA = 64
AAA_LORA = 64
MV_LORA = 32
GATE_LORA = 128
LNX_EPS = 64e-5
ATT_HEADS = 8
ATT_KV_HEADS = 2
ATT_GROUP = ATT_HEADS // ATT_KV_HEADS
ATT_HEAD = 64
ATT_Q_DIM = ATT_HEADS * ATT_HEAD
ATT_KV_DIM = ATT_KV_HEADS * ATT_HEAD
ATT_SCALE = ATT_HEAD ** -0.5
V_BLOCK = 2 * ATT_HEAD
KV_COLS = ATT_KV_DIM + ATT_KV_HEADS * V_BLOCK
WINDOW = 128
ROPE_BASE = 10000.0
ROPE_FREQS = ATT_HEAD // 4
NEG_INF = -1e30
FOURIER_GROUPS = 4
FOURIER_GROUP_DIM = 64
FOURIER_DIM = FOURIER_GROUPS * FOURIER_GROUP_DIM
EPS = 1e-6
RWKV_COLS = 3 * RWKV_DIM + 2 * DECAY_LORA + 2 * AAA_LORA + GATE_LORA

ROW_TILE = 256
ROW_BATCH = 2
HALO = 8
WKV_CHUNK = 64
WKV_BATCH = 4
ATT_BLOCK = 128
FF_CHUNK = 256
VMEM_LIMIT = 56 * 1024 * 1024


def _cparams(n_axes):
    return pltpu.CompilerParams(dimension_semantics=("arbitrary",) * n_axes,
                                vmem_limit_bytes=VMEM_LIMIT)


def _dot(a, b):
    return jnp.dot(a.astype(BF16), b.astype(BF16), preferred_element_type=F32)


def _dot_nt(a, b):
    return lax.dot_general(a.astype(BF16), b.astype(BF16), (((1,), (1,)), ((), ())),
                           preferred_element_type=F32)


def _split3(x):
    hi = x.astype(BF16)
    r1 = x - hi.astype(F32)
    mid = r1.astype(BF16)
    lo = (r1 - mid.astype(F32)).astype(BF16)
    return hi, mid, lo


def _dot_exact_rhs(x, m):
    hi, mid, lo = _split3(x)
    return (jnp.dot(hi, m, preferred_element_type=F32) + jnp.dot(mid, m, preferred_element_type=F32)
            + jnp.dot(lo, m, preferred_element_type=F32))


def _dot_exact_lhs(m, x):
    hi, mid, lo = _split3(x)
    return (jnp.dot(m, hi, preferred_element_type=F32) + jnp.dot(m, mid, preferred_element_type=F32)
            + jnp.dot(m, lo, preferred_element_type=F32))


def _sigmoid(x):
    return 1.0 / (1.0 + jnp.exp(-x))


def _const_spec(shape):
    nd = len(shape)
    return pl.BlockSpec(shape, lambda *_: (0,) * nd, pipeline_mode=pl.Buffered(1))


def _tile_with_halo_specs(nb, tm, d, n_own, first):
    hb = tm // HALO
    own = lambda i: jnp.clip(i - first, 0, n_own - 1)
    return [pl.BlockSpec((nb, tm, d), lambda bb, i: (bb, own(i), 0)),
            pl.BlockSpec((nb, HALO, d), lambda bb, i: (bb, jnp.maximum(own(i) * hb - 1, 0), 0)),
            pl.BlockSpec((nb, HALO, d), lambda bb, i: (bb, jnp.minimum((own(i) + 1) * hb, n_own * hb - 1), 0))]


def _rms_mod(x, g, shift, scale):
    y = x * lax.rsqrt(jnp.mean(x * x, axis=-1, keepdims=True) + EPS) * g
    return y * (1.0 + scale) + shift


def _mod_kernel(c_ref, w_ref, b_ref, o_ref):
    x = c_ref[...]
    o_ref[0] = _dot(x * _sigmoid(x), w_ref[0]) + b_ref[0]


def _modulation(cvec, mod_w, mod_b):
    depth, d, n = mod_w.shape
    rows = cvec.shape[0]
    tn = 1536
    return pl.pallas_call(
        _mod_kernel,
        grid=(depth, n // tn),
        in_specs=[pl.BlockSpec((rows, d), lambda l, j: (0, 0)),
                  pl.BlockSpec((1, d, tn), lambda l, j: (l, 0, j)),
                  pl.BlockSpec((1, 1, tn), lambda l, j: (l, 0, j))],
        out_specs=pl.BlockSpec((1, rows, tn), lambda l, j: (l, 0, j)),
        out_shape=jax.ShapeDtypeStruct((depth, rows, n), F32),
        compiler_params=_cparams(2),
        name="adaln_modulation",
    )(cvec, mod_w.astype(BF16), mod_b.reshape(depth, 1, n))


def _rope(x, cos, sin_signed):
    n = x.shape[1]
    lane = lax.broadcasted_iota(jnp.int32, x.shape, 1)
    first = (lane & ROPE_FREQS) == 0
    partner = jnp.where(first, pltpu.roll(x, n - ROPE_FREQS, 1), pltpu.roll(x, ROPE_FREQS, 1))
    return x * cos + partner * sin_signed


def _inproj_kernel(*refs, split_input, has_vres, n_lat_tiles, n_tiles):
    i = pl.program_id(1)
    if split_input:
        is_lat = i < n_lat_tiles
        lat_refs, ctx_refs = refs[0:3], refs[3:6]
        tile_of = lambda bi, k: jnp.where(is_lat, lat_refs[k][bi], ctx_refs[k][bi])
        refs = refs[3:]
    else:
        lat_refs = refs[0:3]
        tile_of = lambda bi, k: lat_refs[k][bi]
    h_ref = refs[0]
    (mod_ref, g_ref, wr_ref, wq_ref, wkv_ref, wf_ref, wg_ref, dft_ref, cos_ref, sin_ref,
     mu_ref, vec_ref, w0_ref, a0_ref, w2_ref, a2_ref, g2_ref, seg_ref) = refs[3:21]
    refs = refs[21:]
    if has_vres:
        vf_ref, v0_ref, v1_ref, v2_ref = refs[:4]
        refs = refs[4:]
    q_ref, kv_ref, z_ref, ug_ref, fc_ref, ff_ref, fb_ref, ro_ref, buf_ref = refs
    nb, tm, _ = h_ref.shape
    th = tm + 2 * HALO
    prev_ok = jnp.logical_and(i != 0, i != n_lat_tiles)
    next_ok = jnp.logical_and(i != n_lat_tiles - 1, i != n_tiles - 1)
    rid = lax.broadcasted_iota(jnp.int32, (th, 1), 0)
    live = jnp.logical_and(jnp.logical_or(rid >= HALO, prev_ok), jnp.logical_or(rid < HALO + tm, next_ok))
    a_all, a_main = [], []
    for bi in range(nb):
        x = jnp.concatenate([tile_of(bi, 1), tile_of(bi, 0), tile_of(bi, 2)], axis=0)
        a = _rms_mod(x, g_ref[...], mod_ref[bi, 0, 0:1, :], mod_ref[bi, 0, 1:2, :])
        a_all.append(jnp.where(live, a, 0.0).astype(BF16))
        a_main.append(a[HALO:HALO + tm].astype(BF16))
    a_all = jnp.concatenate(a_all, axis=0)
    a = jnp.concatenate(a_main, axis=0)
    cos = jnp.concatenate([cos_ref[...]] * nb, axis=0)
    sin = jnp.concatenate([sin_ref[...]] * nb, axis=0)
    buf_ref[...] = jnp.dot(a_all, wr_ref[...], preferred_element_type=F32).reshape(nb, th, -1)
    q = jnp.dot(a, wq_ref[...], preferred_element_type=F32)
    q_ref[...] = (_rope(q, cos, sin) * ATT_SCALE).astype(BF16).reshape(nb, tm, -1)
    kv = jnp.dot(a, wkv_ref[...], preferred_element_type=F32)
    k = _rope(kv[:, :ATT_KV_DIM], cos[:, :ATT_KV_DIM], sin[:, :ATT_KV_DIM])
    kv_ref[:, :, :ATT_KV_DIM] = k.astype(BF16).reshape(nb, tm, -1)
    v = kv[:, ATT_KV_DIM:]
    low = lax.broadcasted_iota(jnp.int32, v.shape, 1) < ATT_HEAD
    for g in range(ATT_KV_HEADS):
        v_g = v if g == 0 else pltpu.roll(v, ATT_KV_DIM - g * ATT_HEAD, 1)
        kv_ref[:, :, ATT_KV_DIM + g * V_BLOCK:ATT_KV_DIM + (g + 1) * V_BLOCK] = (
            jnp.where(low, v_g, 1.0)[:, :V_BLOCK].astype(BF16).reshape(nb, tm, -1))
    uf = jnp.dot(a, wf_ref[...], preferred_element_type=F32)
    z_ref[...] = _dot(uf, dft_ref[...]).astype(BF16).reshape(nb, tm, -1)
    half = wg_ref.shape[1] // 2
    ug_ref[:, :, :half] = jnp.dot(a, wg_ref[:, :half], preferred_element_type=F32).astype(BF16).reshape(nb, tm, -1)
    _rwkv_features(buf_ref, tm, vf_ref if has_vres else None, (v0_ref, v1_ref, v2_ref) if has_vres else None,
                   mu_ref, vec_ref, w0_ref, a0_ref, w2_ref, a2_ref, g2_ref, seg_ref, fc_ref, ff_ref, fb_ref, ro_ref)
    ug_ref[:, :, half:] = jnp.dot(a, wg_ref[:, half:], preferred_element_type=F32).astype(BF16).reshape(nb, tm, -1)


def _inproj(h, modtab, g, w_in, dft_c, cos_t, sin_t, v_first_src, p, n_lat_tiles):
    split_input = isinstance(h, tuple)
    parts = h if split_input else (h,)
    b, _, d = parts[0].shape
    l = sum(x.shape[1] for x in parts)
    tm = ROW_TILE
    n = RWKV_DIM
    n_tiles = l // tm
    hb = tm // HALO
    n_g = w_in.shape[1] - (RWKV_COLS + ATT_Q_DIM + 2 * ATT_KV_DIM + FOURIER_DIM)
    o = np.cumsum([0, RWKV_COLS, ATT_Q_DIM, 2 * ATT_KV_DIM, FOURIER_DIM, n_g])
    wb = w_in.astype(BF16)
    ws = [wb[:, o[i]:o[i + 1]] for i in range(5)]
    nb = ROW_BATCH
    has_vres = v_first_src is not None
    row = lambda bb, i: (bb, i, 0)
    consts = [g.reshape(1, d)] + ws + [dft_c]
    feat_consts = [p["mu"], p["vec"], p["w0"], p["a0"], p["w2"], p["a2"], p["g2"], p["seg"]]
    in_specs, args, first = [], [], 0
    for x in parts:
        in_specs += _tile_with_halo_specs(nb, tm, d, x.shape[1] // tm, first)
        args += [x, x, x]
        first += x.shape[1] // tm
    in_specs += ([pl.BlockSpec((nb, 1, 6, d), lambda bb, i: (bb, jnp.where(i < n_lat_tiles, 0, 1), 0, 0))]
                 + [_const_spec(x.shape) for x in consts]
                 + [pl.BlockSpec((tm, ATT_Q_DIM), lambda bb, i: (i, 0)),
                    pl.BlockSpec((tm, ATT_Q_DIM), lambda bb, i: (i, 0))]
                 + [_const_spec(x.shape) for x in feat_consts])
    args += [modtab] + consts + [cos_t, sin_t] + feat_consts
    if has_vres:
        vres_consts = [p["v0"], p["v1"], p["v2"]]
        in_specs += [pl.BlockSpec((nb, tm, n), lambda bb, i: (bb, i, 1))] + [_const_spec(x.shape) for x in vres_consts]
        args += [v_first_src] + vres_consts
    outs = [(ATT_Q_DIM, BF16), (KV_COLS, BF16), (2 * FOURIER_DIM, BF16), (n_g, BF16),
            (3 * n, F32), (3 * n, F32), (3 * n, F32), (2 * n, F32)]
    return pl.pallas_call(
        functools.partial(_inproj_kernel, split_input=split_input, has_vres=has_vres, n_lat_tiles=n_lat_tiles,
                          n_tiles=n_tiles),
        grid=(b // nb, n_tiles),
        in_specs=in_specs,
        out_specs=[pl.BlockSpec((nb, tm, w), row) for w, _ in outs],
        out_shape=[jax.ShapeDtypeStruct((b, l, w), dt) for w, dt in outs],
        scratch_shapes=[pltpu.VMEM((nb, tm + 2 * HALO, RWKV_COLS), F32)],
        compiler_params=_cparams(2),
        name="in_projection",
    )(*args)


def _rwkv_features(buf_ref, tm, vf_ref, vres, mu_ref, vec_ref, w0_ref, a0_ref, w2_ref, a2_ref, g2_ref, seg_ref,
                   fc_ref, ff_ref, fb_ref, ro_ref):
    n = RWKV_DIM
    k_k = vec_ref[0:1, :]
    k_a = vec_ref[1:2, :]
    r_k = vec_ref[2:3, :]
    seg = seg_ref[...]
    tiles = []
    for bi in range(buf_ref.shape[0]):
        u = buf_ref[bi, HALO:HALO + tm, :]
        u_prev = buf_ref[bi, HALO - 1:HALO - 1 + tm, :]
        u_next = buf_ref[bi, HALO + 1:HALO + 1 + tm, :]
        us = u + mu_ref[0:1, :] * (u_prev - u) + mu_ref[1:2, :] * (u_next - u)
        o = 3 * n
        t = dict(bi=bi, r=us[:, 0:n], k=us[:, n:2 * n], v=us[:, 2 * n:3 * n])
        t["tw"] = [jnp.tanh(us[:, o + d * DECAY_LORA:o + (d + 1) * DECAY_LORA]).astype(BF16) for d in range(2)]
        o += 2 * DECAY_LORA
        t["xa"] = [us[:, o + d * AAA_LORA:o + (d + 1) * AAA_LORA].astype(BF16) for d in range(2)]
        o += 2 * AAA_LORA
        t["sg"] = _sigmoid(us[:, o:o + GATE_LORA]).astype(BF16)
        t["kk"] = t["k"] * k_k
        t["kk_sq"] = _split3(t["kk"] * t["kk"])
        tiles.append(t)
    for t in tiles:
        t["z"] = [jnp.dot(t["tw"][d], w2_ref[d], preferred_element_type=F32) for d in range(2)]
        t["za"] = [jnp.dot(t["xa"][d], a2_ref[d], preferred_element_type=F32) for d in range(2)]
        t["g"] = jnp.dot(t["sg"], g2_ref[...], preferred_element_type=F32)
        t["ss"] = sum(jnp.dot(part, seg, preferred_element_type=F32) for part in t["kk_sq"])
        if vres is not None:
            t["vv"] = _dot(t["v"], vres[1][...])
    if vres is not None:
        for t in tiles:
            t["vg"] = _dot(t["vv"], vres[2][...])
    for t in tiles:
        bi = t["bi"]
        v = t["v"]
        if vres is not None:
            v = v + (vf_ref[bi] - v) * _sigmoid(vres[0][...] + t["vg"])
        kk = t["kk"] * lax.rsqrt(jnp.maximum(t["ss"], 1e-24))
        keys = []
        for d, out_ref in enumerate((ff_ref, fb_ref)):
            z = w0_ref[d:d + 1, :] + t["z"][d]
            softplus = jnp.maximum(-z, 0.0) + jnp.log(1.0 + jnp.exp(-jnp.abs(z)))
            out_ref[bi, :, 0:n] = -jnp.exp(-softplus - 0.5)
            a = _sigmoid(a0_ref[d:d + 1, :] + t["za"][d])
            key = t["k"] * (1.0 + (a - 1.0) * k_a)
            keys.append(key)
            out_ref[bi, :, n:2 * n] = key
            out_ref[bi, :, 2 * n:3 * n] = kk * a
        t["v"] = v
        t["bonus_in"] = _split3(t["r"] * (0.5 * (keys[0] + keys[1])) * r_k)
        fc_ref[bi, :, 0:n] = t["r"]
        fc_ref[bi, :, n:2 * n] = v
        fc_ref[bi, :, 2 * n:3 * n] = kk
        ro_ref[bi, :, n:2 * n] = t["g"]
    for t in tiles:
        bonus = sum(jnp.dot(part, seg, preferred_element_type=F32) for part in t["bonus_in"])
        ro_ref[t["bi"], :, 0:n] = bonus * t["v"]


def _wkv_masks(c, reverse):
    ti = lax.broadcasted_iota(jnp.int32, (c, c), 0)
    si = lax.broadcasted_iota(jnp.int32, (c, c), 1)
    incl, strict = (si >= ti, si > ti) if reverse else (si <= ti, si < ti)
    levels = []
    for sh in range(int(math.log2(c))):
        bt = lax.shift_right_logical(ti, sh)
        bs = lax.shift_right_logical(si, sh)
        if reverse:
            levels.append(jnp.logical_and((bt & 1) == 0, bs == bt + 1))
        else:
            levels.append(jnp.logical_and((bt & 1) == 1, bs == bt - 1))
    return incl, strict, levels, jnp.where(ti == si, 1.0, 0.0)


def _cumsum_rows(x, reverse):
    rows = x.shape[0]
    t = lax.broadcasted_iota(jnp.int32, x.shape, 0)
    s = 1
    while s < rows:
        if reverse:
            x = x + jnp.where(t < rows - s, pltpu.roll(x, rows - s, 0), 0.0)
        else:
            x = x + jnp.where(t >= s, pltpu.roll(x, s, 0), 0.0)
        s *= 2
    return x


def _wkv_kernel(fcf_ref, ff_ref, fcb_ref, fb_ref, yf_ref, yb_ref, sf_ref, sb_ref):
    @pl.when(pl.program_id(1) == 0)
    def _():
        sf_ref[...] = jnp.zeros_like(sf_ref)
        sb_ref[...] = jnp.zeros_like(sb_ref)

    c = WKV_CHUNK
    n = RWKV_DIM
    hd = RWKV_HEAD
    nb = fcf_ref.shape[0]
    probs = []
    for fc_ref, fd_ref, y_ref, s_ref, reverse in ((fcf_ref, ff_ref, yf_ref, sf_ref, False),
                                                  (fcb_ref, fb_ref, yb_ref, sb_ref, True)):
        incl, strict, levels, eye = _wkv_masks(c, reverse)
        for bi in range(nb):
            cum = _cumsum_rows(fd_ref[bi, :, 0:n], reverse)
            total = cum[0:1, :] if reverse else cum[c - 1:c, :]
            c0 = 0.5 * total
            e_neg = jnp.exp(c0 - cum)
            a_t = -fc_ref[bi, :, 2 * n:3 * n] * jnp.exp(cum - fd_ref[bi, :, 0:n] - c0)
            r_t = fc_ref[bi, :, 0:n] * jnp.exp(cum - c0)
            b_t = fd_ref[bi, :, 2 * n:3 * n] * e_neg
            k_t = fd_ref[bi, :, n:2 * n] * e_neg
            e_half = jnp.exp(c0)
            e_tot = jnp.exp(total)
            for h in range(RWKV_HEADS):
                sl = slice(h * hd, (h + 1) * hd)
                probs.append(dict(
                    lhs=jnp.concatenate([a_t[:, sl], r_t[:, sl]], axis=0).astype(BF16),
                    rhs=jnp.concatenate([b_t[:, sl], k_t[:, sl]], axis=0).astype(BF16),
                    v=fc_ref[bi, :, n + h * hd:n + (h + 1) * hd], s0=s_ref[bi, h],
                    e_half=e_half[:, sl], e_tot=e_tot[:, sl], incl=incl, strict=strict, levels=levels, eye=eye,
                    y_ref=y_ref, s_ref=s_ref, bi=bi, h=h, sl=sl))

    for p in probs:
        p["g"] = _dot_nt(p["lhs"], p["rhs"])
    for p in probs:
        p["a_s"] = _dot_nt(p["lhs"], p["s0"] * p["e_half"])
    for p in probs:
        g = p["g"]
        p["a_ab"] = jnp.where(p["strict"], g[:c, :c], 0.0)
        a_ak = jnp.where(p["strict"], g[:c, c:], 0.0)
        p["a_r"] = jnp.concatenate([jnp.where(p["incl"], g[c:, :c], 0.0),
                                    jnp.where(p["incl"], g[c:, c:], 0.0)], axis=1).astype(BF16)
        p["t"] = p["eye"] + jnp.where(p["levels"][0], p["a_ab"], 0.0)
        p["rhs_u"] = p["a_s"][:c] + _dot(a_ak, p["v"])
    for lvl in range(1, len(probs[0]["levels"])):
        for p in probs:
            p["tb"] = p["t"].astype(BF16)
            p["tmp"] = _dot(jnp.where(p["levels"][lvl], p["a_ab"], 0.0), p["tb"])
        for p in probs:
            p["t"] = p["t"] + _dot(p["tb"], p["tmp"])
    for p in probs:
        p["uv"] = jnp.concatenate([_dot(p["t"], p["rhs_u"]), p["v"]], axis=0).astype(BF16)
    for p in probs:
        p["y_ref"][p["bi"], :, p["sl"]] = p["a_s"][c:] + _dot(p["a_r"], p["uv"])
    for p in probs:
        upd = lax.dot_general(p["uv"], p["rhs"], (((0,), (0,)), ((), ())), preferred_element_type=F32)
        p["s_ref"][p["bi"], p["h"]] = p["s0"] * p["e_tot"] + upd * p["e_half"]


def _wkv_scan(fc, ff, fb, n_lat):
    b, l, w = fc.shape
    c = WKV_CHUNK
    nb = WKV_BATCH
    nl = n_lat // c
    nc = l // c - nl
    fwd = lambda bb, j: (bb, jnp.where(j < nc, nl + j, j - nc), 0)
    bwd = lambda bb, j: (bb, nl + nc - 1 - j, 0)
    return pl.pallas_call(
        _wkv_kernel,
        grid=(b // nb, nl + nc),
        in_specs=[pl.BlockSpec((nb, c, w), fwd), pl.BlockSpec((nb, c, w), fwd),
                  pl.BlockSpec((nb, c, w), bwd), pl.BlockSpec((nb, c, w), bwd)],
        out_specs=[pl.BlockSpec((nb, c, RWKV_DIM), fwd), pl.BlockSpec((nb, c, RWKV_DIM), bwd)],
        out_shape=[jax.ShapeDtypeStruct((b, l, RWKV_DIM), F32)] * 2,
        scratch_shapes=[pltpu.VMEM((nb, RWKV_HEADS, RWKV_HEAD, RWKV_HEAD), F32)] * 2,
        compiler_params=_cparams(2),
        name="wkv_scan",
    )(fc, ff, fc, fb)


def _attn_kernel(sink_ref, q_ref, kv_ref, o_ref, *, n_lat):
    j = pl.program_id(1)
    qb = ATT_BLOCK
    hd = ATT_HEAD
    n_ctx = kv_ref.shape[1] - n_lat
    n_win = 3 * qb
    is_lat = j * qb < n_lat
    ws = pl.multiple_of(jnp.clip((j - 1) * qb, 0, n_lat - n_win), qb)
    col = lax.broadcasted_iota(jnp.int32, (qb, n_ctx + n_win), 1)
    q_pos = j * qb + lax.broadcasted_iota(jnp.int32, (qb, n_ctx + n_win), 0)
    k_pos = ws + col - n_ctx
    valid = jnp.logical_or(col < n_ctx, jnp.logical_and(jnp.abs(q_pos - k_pos) <= WINDOW, is_lat))
    q = q_ref[0]
    kv = jnp.concatenate([kv_ref[0, n_lat:n_lat + n_ctx, :], kv_ref[0, pl.ds(ws, n_win), :]], axis=0)
    heads = range(ATT_HEADS)
    s = [jnp.where(valid, _dot_nt(q[:, h * hd:(h + 1) * hd],
                                  kv[:, (h // ATT_GROUP) * hd:(h // ATT_GROUP + 1) * hd]), NEG_INF) for h in heads]
    m = [jnp.maximum(jnp.max(s[h], axis=-1, keepdims=True), sink_ref[h]) for h in heads]
    p = [jnp.exp((s[h] - m[h]).astype(BF16)) for h in heads]
    o = [jnp.dot(p[h], kv[:, ATT_KV_DIM + (h // ATT_GROUP) * V_BLOCK:ATT_KV_DIM + (h // ATT_GROUP + 1) * V_BLOCK],
                 preferred_element_type=F32) for h in heads]
    for h in heads:
        den = pltpu.roll(o[h], hd, 1) + jnp.exp(sink_ref[h] - m[h])
        o_ref[0, :, h * hd:(h + 1) * hd] = (o[h] / den)[:, :hd].astype(o_ref.dtype)


def _attention(q, kv, sink, n_lat, n_rows):
    b, l, _ = q.shape
    qb = ATT_BLOCK
    return pl.pallas_call(
        functools.partial(_attn_kernel, n_lat=n_lat),
        grid=(b, n_rows // qb),
        in_specs=[pl.BlockSpec(memory_space=pltpu.SMEM),
                  pl.BlockSpec((1, qb, ATT_Q_DIM), lambda bb, j: (bb, j, 0)),
                  pl.BlockSpec((1, l, kv.shape[2]), lambda bb, j: (bb, 0, 0))],
        out_specs=pl.BlockSpec((1, qb, ATT_Q_DIM), lambda bb, j: (bb, j, 0)),
        out_shape=jax.ShapeDtypeStruct((b, n_rows, ATT_Q_DIM), BF16),
        compiler_params=_cparams(2),
        name="windowed_attention",
    )(sink, q, kv)


def _dft_kernel(z_ref, ct_ref, st_ref, o_ref):
    n = FOURIER_DIM
    o_ref[0] = (jnp.dot(ct_ref[...], z_ref[0, :, 0:n], preferred_element_type=F32)
                - jnp.dot(st_ref[...], z_ref[0, :, n:2 * n], preferred_element_type=F32)).astype(o_ref.dtype)


def _token_dft(z, ct, st, seg_rows, seg_block):
    b = z.shape[0]
    return pl.pallas_call(
        _dft_kernel,
        grid=(b,),
        in_specs=[pl.BlockSpec((1, seg_rows, z.shape[2]), lambda bb: (bb, seg_block, 0)),
                  _const_spec(ct.shape), _const_spec(st.shape)],
        out_specs=pl.BlockSpec((1, seg_rows, FOURIER_DIM), lambda bb: (bb, 0, 0)),
        out_shape=jax.ShapeDtypeStruct((b, seg_rows, FOURIER_DIM), BF16),
        compiler_params=_cparams(1),
        name="token_dft",
    )(z, ct, st)


def _merge_kernel(*refs, split_input, n_lat_tiles):
    if split_input:
        is_lat = pl.program_id(1) < n_lat_tiles
        x_ref, c_ref = refs[0:2]
        residual = lambda bi: jnp.where(is_lat, x_ref[bi], c_ref[bi])
        refs = refs[1:]
    else:
        residual = lambda bi: h_ref[bi]
    (h_ref, mod_ref, yf_ref, yb_ref, ro_ref, ya_ref, yd_ref, ug_ref, ln_ref, avg_ref,
     wbr_ref, wba_ref, wbf_ref, wo_ref, gp_ref, o_ref) = refs
    n = RWKV_DIM
    d = h_ref.shape[2]
    avg = avg_ref[...]
    nb, tm, _ = h_ref.shape
    rows = nb * tm
    y = (yf_ref[...] + yb_ref[...]).reshape(rows, n)
    dev = y - _dot_exact_rhs(y, avg)
    var = _dot_exact_rhs(dev * dev, avg)
    yn = dev * lax.rsqrt(var + LNX_EPS) * ln_ref[0:1, :] + ln_ref[1:2, :]
    ro = ro_ref[...].reshape(rows, 2 * n)
    y_r = (yn + ro[:, 0:n]) * ro[:, n:2 * n]
    mix = _sigmoid(ug_ref[:, :, 0:d].reshape(rows, d).astype(F32)) * _dot(y_r, wbr_ref[...])
    mix += (_sigmoid(ug_ref[:, :, d:2 * d].reshape(rows, d).astype(F32))
            * jnp.dot(ya_ref[...].reshape(rows, -1), wba_ref[...], preferred_element_type=F32))
    mix += (_sigmoid(ug_ref[:, :, 2 * d:3 * d].reshape(rows, d).astype(F32))
            * jnp.dot(yd_ref[...].reshape(rows, -1), wbf_ref[...], preferred_element_type=F32))
    o = _dot(mix, wo_ref[...])
    o = o * lax.rsqrt(jnp.mean(o * o, axis=-1, keepdims=True) + EPS) * gp_ref[...]
    for bi in range(nb):
        o_ref[bi] = residual(bi) + mod_ref[bi, 0, 2:3, :] * o[bi * tm:(bi + 1) * tm]


def _merge(h, modtab, yf, yb, ro, ya, yd, ug, p, n_rows, n_lat_tiles):
    split_input = isinstance(h, tuple)
    parts = h if split_input else (h,)
    b, _, d = parts[0].shape
    tm = ROW_TILE
    nb = ROW_BATCH
    row = lambda bb, i: (bb, i, 0)
    consts = [p["ln"], p["avg"], p["wbr"], p["wba"], p["wbf"], p["wo"], p["gpost"]]
    h_specs, first = [], 0
    for x in parts:
        h_specs.append(_tile_with_halo_specs(nb, tm, d, x.shape[1] // tm, first)[0])
        first += x.shape[1] // tm
    return pl.pallas_call(
        functools.partial(_merge_kernel, split_input=split_input, n_lat_tiles=n_lat_tiles),
        grid=(b // nb, n_rows // tm),
        in_specs=h_specs + [
                  pl.BlockSpec((nb, 1, 6, d), lambda bb, i: (bb, jnp.where(i < n_lat_tiles, 0, 1), 0, 0)),
                  pl.BlockSpec((nb, tm, RWKV_DIM), row), pl.BlockSpec((nb, tm, RWKV_DIM), row),
                  pl.BlockSpec((nb, tm, 2 * RWKV_DIM), row),
                  pl.BlockSpec((nb, tm, ATT_Q_DIM), row), pl.BlockSpec((nb, tm, FOURIER_DIM), row),
                  pl.BlockSpec((nb, tm, 3 * d), row)] + [_const_spec(x.shape) for x in consts],
        out_specs=pl.BlockSpec((nb, tm, d), row),
        out_shape=jax.ShapeDtypeStruct((b, n_rows, d), F32),
        compiler_params=_cparams(2),
        name="branch_merge",
    )(*parts, modtab, yf, yb, ro, ya, yd, ug, *consts)


def _ffn_kernel(h_ref, hp_ref, hn_ref, mod_ref, gpre_ref, upg_ref, upv_ref, cw_ref, dn_ref, gpost_ref, o_ref,
                zg_ref, act_ref, *, n_lat_tiles, n_tiles):
    i = pl.program_id(1)
    tm = h_ref.shape[1]
    prev_ok = jnp.logical_and(i != 0, i != n_lat_tiles)
    next_ok = jnp.logical_and(i != n_lat_tiles - 1, i != n_tiles - 1)
    rid = lax.broadcasted_iota(jnp.int32, (tm + 2 * HALO, 1), 0)
    live = jnp.logical_and(jnp.logical_or(rid >= HALO, prev_ok), jnp.logical_or(rid < HALO + tm, next_ok))
    nb = h_ref.shape[0]
    th = tm + 2 * HALO
    f_all, f_main = [], []
    for bi in range(nb):
        x = jnp.concatenate([hp_ref[bi], h_ref[bi], hn_ref[bi]], axis=0)
        f = _rms_mod(x, gpre_ref[...], mod_ref[bi, 0, 3:4, :], mod_ref[bi, 0, 4:5, :])
        f_all.append(jnp.where(live, f, 0.0).astype(BF16))
        f_main.append(f[HALO:HALO + tm].astype(BF16))
    f_all = jnp.concatenate(f_all, axis=0)
    f_main = jnp.concatenate(f_main, axis=0)
    for c in range(zg_ref.shape[2] // FF_CHUNK):
        cols = slice(c * FF_CHUNK, (c + 1) * FF_CHUNK)
        zg_ref[:, :, cols] = jnp.dot(f_all, upg_ref[:, cols], preferred_element_type=F32).reshape(nb, th, -1)
        zv = jnp.dot(f_main, upv_ref[:, cols], preferred_element_type=F32).reshape(nb, tm, -1)
        zg = (cw_ref[0:1, cols] * zg_ref[:, HALO - 1:HALO - 1 + tm, cols]
              + cw_ref[1:2, cols] * zg_ref[:, HALO:HALO + tm, cols]
              + cw_ref[2:3, cols] * zg_ref[:, HALO + 1:HALO + 1 + tm, cols] + cw_ref[3:4, cols])
        act = 0.5 * zg * (1.0 + jnp.tanh(0.7978845608028654 * (zg + 0.044715 * zg * zg * zg)))
        act_ref[:, :, cols] = (act * zv).astype(BF16)
    o = jnp.dot(act_ref[...].reshape(nb * tm, -1), dn_ref[...], preferred_element_type=F32)
    o = o * lax.rsqrt(jnp.mean(o * o, axis=-1, keepdims=True) + EPS) * gpost_ref[...]
    for bi in range(nb):
        o_ref[bi] = h_ref[bi] + mod_ref[bi, 0, 5:6, :] * o[bi * tm:(bi + 1) * tm]


def _ffn(h, modtab, p, n_rows, n_lat_tiles):
    b, l, d = h.shape
    tm = ROW_TILE
    hb = tm // HALO
    n_tiles = l // tm
    d_ff = p["dn"].shape[0]
    nb = ROW_BATCH
    row = lambda bb, i: (bb, i, 0)
    consts = [p["gpre"], p["upg"], p["upv"], p["cw"], p["dn"], p["gpost"]]
    return pl.pallas_call(
        functools.partial(_ffn_kernel, n_lat_tiles=n_lat_tiles, n_tiles=n_tiles),
        grid=(b // nb, n_rows // tm),
        in_specs=[pl.BlockSpec((nb, tm, d), row),
                  pl.BlockSpec((nb, HALO, d), lambda bb, i: (bb, jnp.maximum(i * hb - 1, 0), 0)),
                  pl.BlockSpec((nb, HALO, d), lambda bb, i: (bb, jnp.minimum((i + 1) * hb, l // HALO - 1), 0)),
                  pl.BlockSpec((nb, 1, 6, d), lambda bb, i: (bb, jnp.where(i < n_lat_tiles, 0, 1), 0, 0))]
        + [_const_spec(x.shape) for x in consts],
        out_specs=pl.BlockSpec((nb, tm, d), row),
        out_shape=jax.ShapeDtypeStruct((b, n_rows, d), F32),
        scratch_shapes=[pltpu.VMEM((nb, tm + 2 * HALO, d_ff), F32), pltpu.VMEM((nb, tm, d_ff), BF16)],
        compiler_params=_cparams(2),
        name="conv_ffn",
    )(h, h, h, modtab, *consts)


def _rope_tables(n_lat, n_ctx):
    t = jnp.arange(n_lat)
    row_id = (t // GRID_W).astype(F32)
    col_id = (t % GRID_W).astype(F32)
    inv = ROPE_BASE ** (-jnp.arange(ROPE_FREQS, dtype=F32) / ROPE_FREQS)
    d = np.arange(ATT_HEAD)
    freq = d % ROPE_FREQS
    ang = jnp.where((d // (2 * ROPE_FREQS) == 0)[None, :], row_id[:, None], col_id[:, None]) * inv[freq][None, :]
    sign = np.where((d // ROPE_FREQS) % 2 == 0, -1.0, 1.0).astype(np.float32)
    cos = jnp.concatenate([jnp.cos(ang), jnp.ones((n_ctx, ATT_HEAD), F32)], axis=0)
    sin = jnp.concatenate([jnp.sin(ang) * sign[None, :], jnp.zeros((n_ctx, ATT_HEAD), F32)], axis=0)
    return jnp.tile(cos, (1, ATT_HEADS)), jnp.tile(sin, (1, ATT_HEADS))


def _dft_mats(n):
    r = 1 << (int(math.log2(n)) // 2)
    u = jnp.arange(n, dtype=jnp.int32)[None, :]

    def table(t):
        ang = ((t[:, None] * u) % n).astype(F32) * (2.0 * math.pi / n)
        return jnp.cos(ang), jnp.sin(ang)

    c_hi, s_hi = table(jnp.arange(n // r, dtype=jnp.int32) * r)
    c_lo, s_lo = table(jnp.arange(r, dtype=jnp.int32))
    scale = 1.0 / math.sqrt(n)
    c_hi, s_hi = c_hi[:, None, :] * scale, s_hi[:, None, :] * scale
    cos = (c_hi * c_lo[None] - s_hi * s_lo[None]).reshape(n, n)
    sin = (s_hi * c_lo[None] + c_hi * s_lo[None]).reshape(n, n)
    return cos, sin


def _block_diag(m, groups):
    return jnp.kron(jnp.eye(groups, dtype=m.dtype), m)


def kernel(x, c, ctx, c_ctx, mod_w, mod_b, norm_mix_pre, norm_mix_post, norm_ffn_pre, norm_ffn_post, w_in, rwkv_mu, rwkv_w0, rwkv_w2, rwkv_a0, rwkv_a2, rwkv_g2, rwkv_k_k, rwkv_k_a, rwkv_r_k, rwkv_lnx_w, rwkv_lnx_b, rwkv_v0, rwkv_v1, rwkv_v2, attn_sink, w_branch_rwkv, w_branch_attn, w_branch_fourier, w_out, ffn_up, ffn_conv_w, ffn_conv_b, ffn_down):
    b, n_lat, d = x.shape
    n_ctx = ctx.shape[1]
    depth = mod_w.shape[0]
    d_ff = ffn_down.shape[1]
    l = n_lat + n_ctx
    tm = ROW_TILE
    assert n_lat % tm == 0 and n_ctx % tm == 0 and n_lat % n_ctx == 0
    assert n_lat >= 3 * ATT_BLOCK and d_ff % FF_CHUNK == 0 and b % WKV_BATCH == 0 and b % ROW_BATCH == 0
    n_lat_tiles = n_lat // tm

    cos_t, sin_t = _rope_tables(n_lat, n_ctx)
    cg, sg = _dft_mats(FOURIER_GROUP_DIM)
    dft_c = jnp.concatenate([_block_diag(cg, FOURIER_GROUPS), _block_diag(sg, FOURIER_GROUPS)], axis=1).astype(BF16)
    ct_lat, st_lat = (m.astype(BF16) for m in _dft_mats(n_lat))
    ct_ctx, st_ctx = (m.astype(BF16) for m in _dft_mats(n_ctx))
    seg = _block_diag(jnp.ones((RWKV_HEAD, RWKV_HEAD), F32), RWKV_HEADS).astype(BF16)
    avg = (seg.astype(F32) / RWKV_HEAD).astype(BF16)

    pad = (-(b + 1)) % 8
    cvec = jnp.concatenate([c, c_ctx[None, :], jnp.zeros((pad, d), F32)], axis=0)
    mod = _modulation(cvec, mod_w, mod_b)

    h = (x, ctx)
    v_first = None
    for layer in range(depth):
        last = layer == depth - 1
        lat = mod[layer, :b].reshape(b, 1, 6, d)
        cm = jnp.broadcast_to(mod[layer, b].reshape(1, 1, 6, d), (b, 1, 6, d))
        modtab = jnp.concatenate([lat, cm], axis=1)

        fp = {
            "mu": rwkv_mu[layer],
            "vec": jnp.stack([rwkv_k_k[layer], rwkv_k_a[layer], rwkv_r_k[layer]]),
            "w0": rwkv_w0[layer], "a0": rwkv_a0[layer],
            "w2": rwkv_w2[layer].astype(BF16), "a2": rwkv_a2[layer].astype(BF16),
            "g2": rwkv_g2[layer].astype(BF16), "seg": seg,
        }
        if layer > 0:
            lp = 128 - MV_LORA
            fp["v0"] = rwkv_v0[layer - 1].reshape(1, RWKV_DIM)
            fp["v1"] = jnp.pad(rwkv_v1[layer - 1], ((0, 0), (0, lp))).astype(BF16)
            fp["v2"] = jnp.pad(rwkv_v2[layer - 1], ((0, lp), (0, 0))).astype(BF16)
        q, kv, z, u_g, fc, ff, fb, ro = _inproj(h, modtab, norm_mix_pre[layer], w_in[layer], dft_c, cos_t, sin_t,
                                                v_first, fp, n_lat_tiles)
        if layer == 0:
            v_first = fc
        y_fwd, y_bwd = _wkv_scan(fc, ff, fb, n_lat)

        n_rows = n_lat if last else l
        y_att = _attention(q, kv, attn_sink[layer], n_lat, n_rows)
        y_dft = _token_dft(z, ct_lat, st_lat, n_lat, 0)
        if not last:
            y_dft = jnp.concatenate([y_dft, _token_dft(z, ct_ctx, st_ctx, n_ctx, n_lat // n_ctx)], axis=1)

        mp = {
            "ln": jnp.stack([rwkv_lnx_w[layer], rwkv_lnx_b[layer]]), "avg": avg,
            "wbr": w_branch_rwkv[layer].astype(BF16), "wba": w_branch_attn[layer].astype(BF16),
            "wbf": w_branch_fourier[layer].astype(BF16), "wo": w_out[layer].astype(BF16),
            "gpost": norm_mix_post[layer].reshape(1, d),
        }
        h = _merge(h, modtab, y_fwd, y_bwd, ro, y_att, y_dft, u_g, mp, n_rows, n_lat_tiles)

        up = ffn_up[layer].astype(BF16)
        pp = {
            "gpre": norm_ffn_pre[layer].reshape(1, d),
            "upg": up[:, :d_ff], "upv": up[:, d_ff:],
            "cw": jnp.concatenate([ffn_conv_w[layer], ffn_conv_b[layer][None, :]], axis=0),
            "dn": ffn_down[layer].astype(BF16),
            "gpost": norm_ffn_post[layer].reshape(1, d),
        }
        h = _ffn(h, modtab, pp, n_rows, n_lat_tiles)
    return h
```

```python
import functools
import math

import numpy as np
import jax
import jax.numpy as jnp
from jax import lax
from jax.experimental import pallas as pl
from jax.experimental.pallas import tpu as pltpu

F32 = jnp.float32
BF16 = jnp.bfloat16

GRID_W = 64
RWKV_HEADS = 4
RWKV_HEAD = 64
RWKV_DIM = RWKV_HEADS * RWKV_HEAD
DECAY_LORA = 64
AAA_LORA = 64
MV_LORA = 32
GATE_LORA = 128
LNX_EPS = 64e-5
ATT_HEADS = 8
ATT_KV_HEADS = 2
ATT_GROUP = ATT_HEADS // ATT_KV_HEADS
ATT_HEAD = 64
ATT_Q_DIM = ATT_HEADS * ATT_HEAD
ATT_KV_DIM = ATT_KV_HEADS * ATT_HEAD
ATT_SCALE = ATT_HEAD ** -0.5
V_BLOCK = 2 * ATT_HEAD
KV_COLS = ATT_KV_DIM + ATT_KV_HEADS * V_BLOCK
WINDOW = 128
ROPE_BASE = 10000.0
ROPE_FREQS = ATT_HEAD // 4
NEG_INF = -1e30
FOURIER_GROUPS = 4
FOURIER_GROUP_DIM = 64
FOURIER_DIM = FOURIER_GROUPS * FOURIER_GROUP_DIM
EPS = 1e-6
RWKV_COLS = 3 * RWKV_DIM + 2 * DECAY_LORA + 2 * AAA_LORA + GATE_LORA

ROW_TILE = 256
ROW_BATCH = 2
HALO = 8
WKV_CHUNK = 64
WKV_BATCH = 4
ATT_BLOCK = 128
FF_CHUNK = 256
VMEM_LIMIT = 56 * 1024 * 1024


def _cparams(n_axes):
    return pltpu.CompilerParams(dimension_semantics=("arbitrary",) * n_axes,
                                vmem_limit_bytes=VMEM_LIMIT)


def _dot(a, b):
    return jnp.dot(a.astype(BF16), b.astype(BF16), preferred_element_type=F32)


def _dot_nt(a, b):
    return lax.dot_general(a.astype(BF16), b.astype(BF16), (((1,), (1,)), ((), ())),
                           preferred_element_type=F32)


def _split3(x):
    hi = x.astype(BF16)
    r1 = x - hi.astype(F32)
    mid = r1.astype(BF16)
    lo = (r1 - mid.astype(F32)).astype(BF16)
    return hi, mid, lo


def _dot_exact_rhs(x, m):
    hi, mid, lo = _split3(x)
    return (jnp.dot(hi, m, preferred_element_type=F32) + jnp.dot(mid, m, preferred_element_type=F32)
            + jnp.dot(lo, m, preferred_element_type=F32))


def _dot_exact_lhs(m, x):
    hi, mid, lo = _split3(x)
    return (jnp.dot(m, hi, preferred_element_type=F32) + jnp.dot(m, mid, preferred_element_type=F32)
            + jnp.dot(m, lo, preferred_element_type=F32))


def _sigmoid(x):
    return 1.0 / (1.0 + jnp.exp(-x))


def _const_spec(shape):
    nd = len(shape)
    return pl.BlockSpec(shape, lambda *_: (0,) * nd, pipeline_mode=pl.Buffered(1))


def _tile_with_halo_specs(nb, tm, d, n_own, first):
    hb = tm // HALO
    own = lambda i: jnp.clip(i - first, 0, n_own - 1)
    return [pl.BlockSpec((nb, tm, d), lambda bb, i: (bb, own(i), 0)),
            pl.BlockSpec((nb, HALO, d), lambda bb, i: (bb, jnp.maximum(own(i) * hb - 1, 0), 0)),
            pl.BlockSpec((nb, HALO, d), lambda bb, i: (bb, jnp.minimum((own(i) + 1) * hb, n_own * hb - 1), 0))]


def _rms_mod(x, g, shift, scale):
    y = x * lax.rsqrt(jnp.mean(x * x, axis=-1, keepdims=True) + EPS) * g
    return y * (1.0 + scale) + shift


def _mod_kernel(c_ref, w_ref, b_ref, o_ref):
    x = c_ref[...]
    o_ref[0] = _dot(x * _sigmoid(x), w_ref[0]) + b_ref[0]


def _modulation(cvec, mod_w, mod_b):
    depth, d, n = mod_w.shape
    rows = cvec.shape[0]
    tn = 1536
    return pl.pallas_call(
        _mod_kernel,
        grid=(depth, n // tn),
        in_specs=[pl.BlockSpec((rows, d), lambda l, j: (0, 0)),
                  pl.BlockSpec((1, d, tn), lambda l, j: (l, 0, j)),
                  pl.BlockSpec((1, 1, tn), lambda l, j: (l, 0, j))],
        out_specs=pl.BlockSpec((1, rows, tn), lambda l, j: (l, 0, j)),
        out_shape=jax.ShapeDtypeStruct((depth, rows, n), F32),
        compiler_params=_cparams(2),
        name="adaln_modulation",
    )(cvec, mod_w.astype(BF16), mod_b.reshape(depth, 1, n))


def _rope(x, cos, sin_signed):
    n = x.shape[1]
    lane = lax.broadcasted_iota(jnp.int32, x.shape, 1)
    first = (lane & ROPE_FREQS) == 0
    partner = jnp.where(first, pltpu.roll(x, n - ROPE_FREQS, 1), pltpu.roll(x, ROPE_FREQS, 1))
    return x * cos + partner * sin_signed


def _inproj_kernel(*refs, split_input, has_vres, n_lat_tiles, n_tiles):
    i = pl.program_id(1)
    if split_input:
        is_lat = i < n_lat_tiles
        lat_refs, ctx_refs = refs[0:3], refs[3:6]
        tile_of = lambda bi, k: jnp.where(is_lat, lat_refs[k][bi], ctx_refs[k][bi])
        refs = refs[3:]
    else:
        lat_refs = refs[0:3]
        tile_of = lambda bi, k: lat_refs[k][bi]
    h_ref = refs[0]
    (mod_ref, g_ref, wr_ref, wq_ref, wkv_ref, wf_ref, wg_ref, dft_ref, cos_ref, sin_ref,
     mu_ref, vec_ref, w0_ref, a0_ref, w2_ref, a2_ref, g2_ref, seg_ref) = refs[3:21]
    refs = refs[21:]
    if has_vres:
        vf_ref, v0_ref, v1_ref, v2_ref = refs[:4]
        refs = refs[4:]
    q_ref, kv_ref, z_ref, ug_ref, fc_ref, ff_ref, fb_ref, ro_ref, buf_ref = refs
    nb, tm, _ = h_ref.shape
    th = tm + 2 * HALO
    prev_ok = jnp.logical_and(i != 0, i != n_lat_tiles)
    next_ok = jnp.logical_and(i != n_lat_tiles - 1, i != n_tiles - 1)
    rid = lax.broadcasted_iota(jnp.int32, (th, 1), 0)
    live = jnp.logical_and(jnp.logical_or(rid >= HALO, prev_ok), jnp.logical_or(rid < HALO + tm, next_ok))
    a_all, a_main = [], []
    for bi in range(nb):
        x = jnp.concatenate([tile_of(bi, 1), tile_of(bi, 0), tile_of(bi, 2)], axis=0)
        a = _rms_mod(x, g_ref[...], mod_ref[bi, 0, 0:1, :], mod_ref[bi, 0, 1:2, :])
        a_all.append(jnp.where(live, a, 0.0).astype(BF16))
        a_main.append(a[HALO:HALO + tm].astype(BF16))
    a_all = jnp.concatenate(a_all, axis=0)
    a = jnp.concatenate(a_main, axis=0)
    cos = jnp.concatenate([cos_ref[...]] * nb, axis=0)
    sin = jnp.concatenate([sin_ref[...]] * nb, axis=0)
    buf_ref[...] = jnp.dot(a_all, wr_ref[...], preferred_element_type=F32).reshape(nb, th, -1)
    q = jnp.dot(a, wq_ref[...], preferred_element_type=F32)
    q_ref[...] = (_rope(q, cos, sin) * ATT_SCALE).astype(BF16).reshape(nb, tm, -1)
    kv = jnp.dot(a, wkv_ref[...], preferred_element_type=F32)
    k = _rope(kv[:, :ATT_KV_DIM], cos[:, :ATT_KV_DIM], sin[:, :ATT_KV_DIM])
    kv_ref[:, :, :ATT_KV_DIM] = k.astype(BF16).reshape(nb, tm, -1)
    v = kv[:, ATT_KV_DIM:]
    low = lax.broadcasted_iota(jnp.int32, v.shape, 1) < ATT_HEAD
    for g in range(ATT_KV_HEADS):
        v_g = v if g == 0 else pltpu.roll(v, ATT_KV_DIM - g * ATT_HEAD, 1)
        kv_ref[:, :, ATT_KV_DIM + g * V_BLOCK:ATT_KV_DIM + (g + 1) * V_BLOCK] = (
            jnp.where(low, v_g, 1.0)[:, :V_BLOCK].astype(BF16).reshape(nb, tm, -1))
    uf = jnp.dot(a, wf_ref[...], preferred_element_type=F32)
    z_ref[...] = _dot(uf, dft_ref[...]).astype(BF16).reshape(nb, tm, -1)
    half = wg_ref.shape[1] // 2
    ug_ref[:, :, :half] = jnp.dot(a, wg_ref[:, :half], preferred_element_type=F32).astype(BF16).reshape(nb, tm, -1)
    _rwkv_features(buf_ref, tm, vf_ref if has_vres else None, (v0_ref, v1_ref, v2_ref) if has_vres else None,
                   mu_ref, vec_ref, w0_ref, a0_ref, w2_ref, a2_ref, g2_ref, seg_ref, fc_ref, ff_ref, fb_ref, ro_ref)
    ug_ref[:, :, half:] = jnp.dot(a, wg_ref[:, half:], preferred_element_type=F32).astype(BF16).reshape(nb, tm, -1)


def _inproj(h, modtab, g, w_in, dft_c, cos_t, sin_t, v_first_src, p, n_lat_tiles):
    split_input = isinstance(h, tuple)
    parts = h if split_input else (h,)
    b, _, d = parts[0].shape
    l = sum(x.shape[1] for x in parts)
    tm = ROW_TILE
    n = RWKV_DIM
    n_tiles = l // tm
    hb = tm // HALO
    n_g = w_in.shape[1] - (RWKV_COLS + ATT_Q_DIM + 2 * ATT_KV_DIM + FOURIER_DIM)
    o = np.cumsum([0, RWKV_COLS, ATT_Q_DIM, 2 * ATT_KV_DIM, FOURIER_DIM, n_g])
    wb = w_in.astype(BF16)
    ws = [wb[:, o[i]:o[i + 1]] for i in range(5)]
    nb = ROW_BATCH
    has_vres = v_first_src is not None
    row = lambda bb, i: (bb, i, 0)
    consts = [g.reshape(1, d)] + ws + [dft_c]
    feat_consts = [p["mu"], p["vec"], p["w0"], p["a0"], p["w2"], p["a2"], p["g2"], p["seg"]]
    in_specs, args, first = [], [], 0
    for x in parts:
        in_specs += _tile_with_halo_specs(nb, tm, d, x.shape[1] // tm, first)
        args += [x, x, x]
        first += x.shape[1] // tm
    in_specs += ([pl.BlockSpec((nb, 1, 6, d), lambda bb, i: (bb, jnp.where(i < n_lat_tiles, 0, 1), 0, 0))]
                 + [_const_spec(x.shape) for x in consts]
                 + [pl.BlockSpec((tm, ATT_Q_DIM), lambda bb, i: (i, 0)),
                    pl.BlockSpec((tm, ATT_Q_DIM), lambda bb, i: (i, 0))]
                 + [_const_spec(x.shape) for x in feat_consts])
    args += [modtab] + consts + [cos_t, sin_t] + feat_consts
    if has_vres:
        vres_consts = [p["v0"], p["v1"], p["v2"]]
        in_specs += [pl.BlockSpec((nb, tm, n), lambda bb, i: (bb, i, 1))] + [_const_spec(x.shape) for x in vres_consts]
        args += [v_first_src] + vres_consts
    outs = [(ATT_Q_DIM, BF16), (KV_COLS, BF16), (2 * FOURIER_DIM, BF16), (n_g, BF16),
            (3 * n, F32), (3 * n, F32), (3 * n, F32), (2 * n, F32)]
    return pl.pallas_call(
        functools.partial(_inproj_kernel, split_input=split_input, has_vres=has_vres, n_lat_tiles=n_lat_tiles,
                          n_tiles=n_tiles),
        grid=(b // nb, n_tiles),
        in_specs=in_specs,
        out_specs=[pl.BlockSpec((nb, tm, w), row) for w, _ in outs],
        out_shape=[jax.ShapeDtypeStruct((b, l, w), dt) for w, dt in outs],
        scratch_shapes=[pltpu.VMEM((nb, tm + 2 * HALO, RWKV_COLS), F32)],
        compiler_params=_cparams(2),
        name="in_projection",
    )(*args)


def _rwkv_features(buf_ref, tm, vf_ref, vres, mu_ref, vec_ref, w0_ref, a0_ref, w2_ref, a2_ref, g2_ref, seg_ref,
                   fc_ref, ff_ref, fb_ref, ro_ref):
    n = RWKV_DIM
    k_k = vec_ref[0:1, :]
    k_a = vec_ref[1:2, :]
    r_k = vec_ref[2:3, :]
    seg = seg_ref[...]
    tiles = []
    for bi in range(buf_ref.shape[0]):
        u = buf_ref[bi, HALO:HALO + tm, :]
        u_prev = buf_ref[bi, HALO - 1:HALO - 1 + tm, :]
        u_next = buf_ref[bi, HALO + 1:HALO + 1 + tm, :]
        us = u + mu_ref[0:1, :] * (u_prev - u) + mu_ref[1:2, :] * (u_next - u)
        o = 3 * n
        t = dict(bi=bi, r=us[:, 0:n], k=us[:, n:2 * n], v=us[:, 2 * n:3 * n])
        t["tw"] = [jnp.tanh(us[:, o + d * DECAY_LORA:o + (d + 1) * DECAY_LORA]).astype(BF16) for d in range(2)]
        o += 2 * DECAY_LORA
        t["xa"] = [us[:, o + d * AAA_LORA:o + (d + 1) * AAA_LORA].astype(BF16) for d in range(2)]
        o += 2 * AAA_LORA
        t["sg"] = _sigmoid(us[:, o:o + GATE_LORA]).astype(BF16)
        t["kk"] = t["k"] * k_k
        t["kk_sq"] = _split3(t["kk"] * t["kk"])
        tiles.append(t)
    for t in tiles:
        t["z"] = [jnp.dot(t["tw"][d], w2_ref[d], preferred_element_type=F32) for d in range(2)]
        t["za"] = [jnp.dot(t["xa"][d], a2_ref[d], preferred_element_type=F32) for d in range(2)]
        t["g"] = jnp.dot(t["sg"], g2_ref[...], preferred_element_type=F32)
        t["ss"] = sum(jnp.dot(part, seg, preferred_element_type=F32) for part in t["kk_sq"])
        if vres is not None:
            t["vv"] = _dot(t["v"], vres[1][...])
    if vres is not None:
        for t in tiles:
            t["vg"] = _dot(t["vv"], vres[2][...])
    for t in tiles:
        bi = t["bi"]
        v = t["v"]
        if vres is not None:
            v = v + (vf_ref[bi] - v) * _sigmoid(vres[0][...] + t["vg"])
        kk = t["kk"] * lax.rsqrt(jnp.maximum(t["ss"], 1e-24))
        keys = []
        for d, out_ref in enumerate((ff_ref, fb_ref)):
            z = w0_ref[d:d + 1, :] + t["z"][d]
            softplus = jnp.maximum(-z, 0.0) + jnp.log(1.0 + jnp.exp(-jnp.abs(z)))
            out_ref[bi, :, 0:n] = -jnp.exp(-softplus - 0.5)
            a = _sigmoid(a0_ref[d:d + 1, :] + t["za"][d])
            key = t["k"] * (1.0 + (a - 1.0) * k_a)
            keys.append(key)
            out_ref[bi, :, n:2 * n] = key
            out_ref[bi, :, 2 * n:3 * n] = kk * a
        t["v"] = v
        t["bonus_in"] = _split3(t["r"] * (0.5 * (keys[0] + keys[1])) * r_k)
        fc_ref[bi, :, 0:n] = t["r"]
        fc_ref[bi, :, n:2 * n] = v
        fc_ref[bi, :, 2 * n:3 * n] = kk
        ro_ref[bi, :, n:2 * n] = t["g"]
    for t in tiles:
        bonus = sum(jnp.dot(part, seg, preferred_element_type=F32) for part in t["bonus_in"])
        ro_ref[t["bi"], :, 0:n] = bonus * t["v"]


def _wkv_masks(c, reverse):
    ti = lax.broadcasted_iota(jnp.int32, (c, c), 0)
    si = lax.broadcasted_iota(jnp.int32, (c, c), 1)
    incl, strict = (si >= ti, si > ti) if reverse else (si <= ti, si < ti)
    levels = []
    for sh in range(int(math.log2(c))):
        bt = lax.shift_right_logical(ti, sh)
        bs = lax.shift_right_logical(si, sh)
        if reverse:
            levels.append(jnp.logical_and((bt & 1) == 0, bs == bt + 1))
        else:
            levels.append(jnp.logical_and((bt & 1) == 1, bs == bt - 1))
    return incl, strict, levels, jnp.where(ti == si, 1.0, 0.0)


def _wkv_kernel(fcf_ref, ff_ref, fcb_ref, fb_ref, yf_ref, yb_ref, sf_ref, sb_ref):
    @pl.when(pl.program_id(1) == 0)
    def _():
        sf_ref[...] = jnp.zeros_like(sf_ref)
        sb_ref[...] = jnp.zeros_like(sb_ref)

    c = WKV_CHUNK
    n = RWKV_DIM
    hd = RWKV_HEAD
    nb = fcf_ref.shape[0]
    probs = []
    for fc_ref, fd_ref, y_ref, s_ref, reverse in ((fcf_ref, ff_ref, yf_ref, sf_ref, False),
                                                  (fcb_ref, fb_ref, yb_ref, sb_ref, True)):
        incl, strict, levels, eye = _wkv_masks(c, reverse)
        tri = jnp.where(incl, 1.0, 0.0).astype(BF16)
        cums = [_dot_exact_lhs(tri, fd_ref[bi, :, 0:n]) for bi in range(nb)]
        for bi in range(nb):
            cum = cums[bi]
            total = cum[0:1, :] if reverse else cum[c - 1:c, :]
            c0 = 0.5 * total
            e_neg = jnp.exp(c0 - cum)
            a_t = -fc_ref[bi, :, 2 * n:3 * n] * jnp.exp(cum - fd_ref[bi, :, 0:n] - c0)
            r_t = fc_ref[bi, :, 0:n] * jnp.exp(cum - c0)
            b_t = fd_ref[bi, :, 2 * n:3 * n] * e_neg
            k_t = fd_ref[bi, :, n:2 * n] * e_neg
            e_half = jnp.exp(c0)
            e_tot = jnp.exp(total)
            for h in range(RWKV_HEADS):
                sl = slice(h * hd, (h + 1) * hd)
                probs.append(dict(
                    lhs=jnp.concatenate([a_t[:, sl], r_t[:, sl]], axis=0).astype(BF16),
                    rhs=jnp.concatenate([b_t[:, sl], k_t[:, sl]], axis=0).astype(BF16),
                    v=fc_ref[bi, :, n + h * hd:n + (h + 1) * hd], s0=s_ref[bi, h],
                    e_half=e_half[:, sl], e_tot=e_tot[:, sl], incl=incl, strict=strict, levels=levels, eye=eye,
                    y_ref=y_ref, s_ref=s_ref, bi=bi, h=h, sl=sl))

    for p in probs:
        p["g"] = _dot_nt(p["lhs"], p["rhs"])
    for p in probs:
        p["a_s"] = _dot_nt(p["lhs"], p["s0"] * p["e_half"])
    for p in probs:
        g = p["g"]
        p["a_ab"] = jnp.where(p["strict"], g[:c, :c], 0.0)
        a_ak = jnp.where(p["strict"], g[:c, c:], 0.0)
        p["a_r"] = jnp.concatenate([jnp.where(p["incl"], g[c:, :c], 0.0),
                                    jnp.where(p["incl"], g[c:, c:], 0.0)], axis=1).astype(BF16)
        p["t"] = p["eye"] + jnp.where(p["levels"][0], p["a_ab"], 0.0)
        p["rhs_u"] = p["a_s"][:c] + _dot(a_ak, p["v"])
    for lvl in range(1, len(probs[0]["levels"])):
        for p in probs:
            p["tb"] = p["t"].astype(BF16)
            p["tmp"] = _dot(jnp.where(p["levels"][lvl], p["a_ab"], 0.0), p["tb"])
        for p in probs:
            p["t"] = p["t"] + _dot(p["tb"], p["tmp"])
    for p in probs:
        p["uv"] = jnp.concatenate([_dot(p["t"], p["rhs_u"]), p["v"]], axis=0).astype(BF16)
    for p in probs:
        p["y_ref"][p["bi"], :, p["sl"]] = p["a_s"][c:] + _dot(p["a_r"], p["uv"])
    for p in probs:
        upd = lax.dot_general(p["uv"], p["rhs"], (((0,), (0,)), ((), ())), preferred_element_type=F32)
        p["s_ref"][p["bi"], p["h"]] = p["s0"] * p["e_tot"] + upd * p["e_half"]


def _wkv_scan(fc, ff, fb, n_lat):
    b, l, w = fc.shape
    c = WKV_CHUNK
    nb = WKV_BATCH
    nl = n_lat // c
    nc = l // c - nl
    fwd = lambda bb, j: (bb, jnp.where(j < nc, nl + j, j - nc), 0)
    bwd = lambda bb, j: (bb, nl + nc - 1 - j, 0)
    return pl.pallas_call(
        _wkv_kernel,
        grid=(b // nb, nl + nc),
        in_specs=[pl.BlockSpec((nb, c, w), fwd), pl.BlockSpec((nb, c, w), fwd),
                  pl.BlockSpec((nb, c, w), bwd), pl.BlockSpec((nb, c, w), bwd)],
        out_specs=[pl.BlockSpec((nb, c, RWKV_DIM), fwd), pl.BlockSpec((nb, c, RWKV_DIM), bwd)],
        out_shape=[jax.ShapeDtypeStruct((b, l, RWKV_DIM), F32)] * 2,
        scratch_shapes=[pltpu.VMEM((nb, RWKV_HEADS, RWKV_HEAD, RWKV_HEAD), F32)] * 2,
        compiler_params=_cparams(2),
        name="wkv_scan",
    )(fc, ff, fc, fb)


def _attn_kernel(sink_ref, q_ref, kv_ref, o_ref, *, n_lat):
    j = pl.program_id(1)
    qb = ATT_BLOCK
    hd = ATT_HEAD
    n_ctx = kv_ref.shape[1] - n_lat
    n_win = 3 * qb
    is_lat = j * qb < n_lat
    ws = pl.multiple_of(jnp.clip((j - 1) * qb, 0, n_lat - n_win), qb)
    col = lax.broadcasted_iota(jnp.int32, (qb, n_ctx + n_win), 1)
    q_pos = j * qb + lax.broadcasted_iota(jnp.int32, (qb, n_ctx + n_win), 0)
    k_pos = ws + col - n_ctx
    valid = jnp.logical_or(col < n_ctx, jnp.logical_and(jnp.abs(q_pos - k_pos) <= WINDOW, is_lat))
    q = q_ref[0]
    kv = jnp.concatenate([kv_ref[0, n_lat:n_lat + n_ctx, :], kv_ref[0, pl.ds(ws, n_win), :]], axis=0)
    heads = range(ATT_HEADS)
    s = [jnp.where(valid, _dot_nt(q[:, h * hd:(h + 1) * hd],
                                  kv[:, (h // ATT_GROUP) * hd:(h // ATT_GROUP + 1) * hd]), NEG_INF) for h in heads]
    m = [jnp.maximum(jnp.max(s[h], axis=-1, keepdims=True), sink_ref[h]) for h in heads]
    p = [jnp.exp((s[h] - m[h]).astype(BF16)) for h in heads]
    o = [jnp.dot(p[h], kv[:, ATT_KV_DIM + (h // ATT_GROUP) * V_BLOCK:ATT_KV_DIM + (h // ATT_GROUP + 1) * V_BLOCK],
                 preferred_element_type=F32) for h in heads]
    for h in heads:
        den = pltpu.roll(o[h], hd, 1) + jnp.exp(sink_ref[h] - m[h])
        o_ref[0, :, h * hd:(h + 1) * hd] = (o[h] / den)[:, :hd].astype(o_ref.dtype)


def _attention(q, kv, sink, n_lat, n_rows):
    b, l, _ = q.shape
    qb = ATT_BLOCK
    return pl.pallas_call(
        functools.partial(_attn_kernel, n_lat=n_lat),
        grid=(b, n_rows // qb),
        in_specs=[pl.BlockSpec(memory_space=pltpu.SMEM),
                  pl.BlockSpec((1, qb, ATT_Q_DIM), lambda bb, j: (bb, j, 0)),
                  pl.BlockSpec((1, l, kv.shape[2]), lambda bb, j: (bb, 0, 0))],
        out_specs=pl.BlockSpec((1, qb, ATT_Q_DIM), lambda bb, j: (bb, j, 0)),
        out_shape=jax.ShapeDtypeStruct((b, n_rows, ATT_Q_DIM), BF16),
        compiler_params=_cparams(2),
        name="windowed_attention",
    )(sink, q, kv)


def _dft_kernel(z_ref, ct_ref, st_ref, o_ref):
    n = FOURIER_DIM
    o_ref[0] = (jnp.dot(ct_ref[...], z_ref[0, :, 0:n], preferred_element_type=F32)
                - jnp.dot(st_ref[...], z_ref[0, :, n:2 * n], preferred_element_type=F32)).astype(o_ref.dtype)


def _token_dft(z, ct, st, seg_rows, seg_block):
    b = z.shape[0]
    return pl.pallas_call(
        _dft_kernel,
        grid=(b,),
        in_specs=[pl.BlockSpec((1, seg_rows, z.shape[2]), lambda bb: (bb, seg_block, 0)),
                  _const_spec(ct.shape), _const_spec(st.shape)],
        out_specs=pl.BlockSpec((1, seg_rows, FOURIER_DIM), lambda bb: (bb, 0, 0)),
        out_shape=jax.ShapeDtypeStruct((b, seg_rows, FOURIER_DIM), BF16),
        compiler_params=_cparams(1),
        name="token_dft",
    )(z, ct, st)


def _merge_kernel(*refs, split_input, n_lat_tiles):
    if split_input:
        is_lat = pl.program_id(1) < n_lat_tiles
        x_ref, c_ref = refs[0:2]
        residual = lambda bi: jnp.where(is_lat, x_ref[bi], c_ref[bi])
        refs = refs[1:]
    else:
        residual = lambda bi: h_ref[bi]
    (h_ref, mod_ref, yf_ref, yb_ref, ro_ref, ya_ref, yd_ref, ug_ref, ln_ref, avg_ref,
     wbr_ref, wba_ref, wbf_ref, wo_ref, gp_ref, o_ref) = refs
    n = RWKV_DIM
    d = h_ref.shape[2]
    avg = avg_ref[...]
    nb, tm, _ = h_ref.shape
    rows = nb * tm
    y = (yf_ref[...] + yb_ref[...]).reshape(rows, n)
    mean = _dot_exact_rhs(y, avg)
    mix = (_sigmoid(ug_ref[:, :, d:2 * d].reshape(rows, d).astype(F32))
           * jnp.dot(ya_ref[...].reshape(rows, -1), wba_ref[...], preferred_element_type=F32))
    dev = y - mean
    var = _dot_exact_rhs(dev * dev, avg)
    mix += (_sigmoid(ug_ref[:, :, 2 * d:3 * d].reshape(rows, d).astype(F32))
            * jnp.dot(yd_ref[...].reshape(rows, -1), wbf_ref[...], preferred_element_type=F32))
    yn = dev * lax.rsqrt(var + LNX_EPS) * ln_ref[0:1, :] + ln_ref[1:2, :]
    ro = ro_ref[...].reshape(rows, 2 * n)
    y_r = (yn + ro[:, 0:n]) * ro[:, n:2 * n]
    mix += _sigmoid(ug_ref[:, :, 0:d].reshape(rows, d).astype(F32)) * _dot(y_r, wbr_ref[...])
    o = _dot(mix, wo_ref[...])
    o = o * lax.rsqrt(jnp.mean(o * o, axis=-1, keepdims=True) + EPS) * gp_ref[...]
    for bi in range(nb):
        o_ref[bi] = residual(bi) + mod_ref[bi, 0, 2:3, :] * o[bi * tm:(bi + 1) * tm]


def _merge(h, modtab, yf, yb, ro, ya, yd, ug, p, n_rows, n_lat_tiles):
    split_input = isinstance(h, tuple)
    parts = h if split_input else (h,)
    b, _, d = parts[0].shape
    tm = ROW_TILE
    nb = ROW_BATCH
    row = lambda bb, i: (bb, i, 0)
    consts = [p["ln"], p["avg"], p["wbr"], p["wba"], p["wbf"], p["wo"], p["gpost"]]
    h_specs, first = [], 0
    for x in parts:
        h_specs.append(_tile_with_halo_specs(nb, tm, d, x.shape[1] // tm, first)[0])
        first += x.shape[1] // tm
    return pl.pallas_call(
        functools.partial(_merge_kernel, split_input=split_input, n_lat_tiles=n_lat_tiles),
        grid=(b // nb, n_rows // tm),
        in_specs=h_specs + [
                  pl.BlockSpec((nb, 1, 6, d), lambda bb, i: (bb, jnp.where(i < n_lat_tiles, 0, 1), 0, 0)),
                  pl.BlockSpec((nb, tm, RWKV_DIM), row), pl.BlockSpec((nb, tm, RWKV_DIM), row),
                  pl.BlockSpec((nb, tm, 2 * RWKV_DIM), row),
                  pl.BlockSpec((nb, tm, ATT_Q_DIM), row), pl.BlockSpec((nb, tm, FOURIER_DIM), row),
                  pl.BlockSpec((nb, tm, 3 * d), row)] + [_const_spec(x.shape) for x in consts],
        out_specs=pl.BlockSpec((nb, tm, d), row),
        out_shape=jax.ShapeDtypeStruct((b, n_rows, d), F32),
        compiler_params=_cparams(2),
        name="branch_merge",
    )(*parts, modtab, yf, yb, ro, ya, yd, ug, *consts)


def _ffn_kernel(h_ref, hp_ref, hn_ref, mod_ref, gpre_ref, upg_ref, upv_ref, cw_ref, dn_ref, gpost_ref, o_ref,
                zg_ref, act_ref, *, n_lat_tiles, n_tiles):
    i = pl.program_id(1)
    tm = h_ref.shape[1]
    prev_ok = jnp.logical_and(i != 0, i != n_lat_tiles)
    next_ok = jnp.logical_and(i != n_lat_tiles - 1, i != n_tiles - 1)
    rid = lax.broadcasted_iota(jnp.int32, (tm + 2 * HALO, 1), 0)
    live = jnp.logical_and(jnp.logical_or(rid >= HALO, prev_ok), jnp.logical_or(rid < HALO + tm, next_ok))
    nb = h_ref.shape[0]
    th = tm + 2 * HALO
    f_all, f_main = [], []
    for bi in range(nb):
        x = jnp.concatenate([hp_ref[bi], h_ref[bi], hn_ref[bi]], axis=0)
        f = _rms_mod(x, gpre_ref[...], mod_ref[bi, 0, 3:4, :], mod_ref[bi, 0, 4:5, :])
        f_all.append(jnp.where(live, f, 0.0).astype(BF16))
        f_main.append(f[HALO:HALO + tm].astype(BF16))
    f_all = jnp.concatenate(f_all, axis=0)
    f_main = jnp.concatenate(f_main, axis=0)
    for c in range(zg_ref.shape[2] // FF_CHUNK):
        cols = slice(c * FF_CHUNK, (c + 1) * FF_CHUNK)
        zg_ref[:, :, cols] = jnp.dot(f_all, upg_ref[:, cols], preferred_element_type=F32).reshape(nb, th, -1)
        zv = jnp.dot(f_main, upv_ref[:, cols], preferred_element_type=F32).reshape(nb, tm, -1)
        zg = (cw_ref[0:1, cols] * zg_ref[:, HALO - 1:HALO - 1 + tm, cols]
              + cw_ref[1:2, cols] * zg_ref[:, HALO:HALO + tm, cols]
              + cw_ref[2:3, cols] * zg_ref[:, HALO + 1:HALO + 1 + tm, cols] + cw_ref[3:4, cols])
        act = 0.5 * zg * (1.0 + jnp.tanh(0.7978845608028654 * (zg + 0.044715 * zg * zg * zg)))
        act_ref[:, :, cols] = (act * zv).astype(BF16)
    o = jnp.dot(act_ref[...].reshape(nb * tm, -1), dn_ref[...], preferred_element_type=F32)
    o = o * lax.rsqrt(jnp.mean(o * o, axis=-1, keepdims=True) + EPS) * gpost_ref[...]
    for bi in range(nb):
        o_ref[bi] = h_ref[bi] + mod_ref[bi, 0, 5:6, :] * o[bi * tm:(bi + 1) * tm]


def _ffn(h, modtab, p, n_rows, n_lat_tiles):
    b, l, d = h.shape
    tm = ROW_TILE
    hb = tm // HALO
    n_tiles = l // tm
    d_ff = p["dn"].shape[0]
    nb = ROW_BATCH
    row = lambda bb, i: (bb, i, 0)
    consts = [p["gpre"], p["upg"], p["upv"], p["cw"], p["dn"], p["gpost"]]
    return pl.pallas_call(
        functools.partial(_ffn_kernel, n_lat_tiles=n_lat_tiles, n_tiles=n_tiles),
        grid=(b // nb, n_rows // tm),
        in_specs=[pl.BlockSpec((nb, tm, d), row),
                  pl.BlockSpec((nb, HALO, d), lambda bb, i: (bb, jnp.maximum(i * hb - 1, 0), 0)),
                  pl.BlockSpec((nb, HALO, d), lambda bb, i: (bb, jnp.minimum((i + 1) * hb, l // HALO - 1), 0)),
                  pl.BlockSpec((nb, 1, 6, d), lambda bb, i: (bb, jnp.where(i < n_lat_tiles, 0, 1), 0, 0))]
        + [_const_spec(x.shape) for x in consts],
        out_specs=pl.BlockSpec((nb, tm, d), row),
        out_shape=jax.ShapeDtypeStruct((b, n_rows, d), F32),
        scratch_shapes=[pltpu.VMEM((nb, tm + 2 * HALO, d_ff), F32), pltpu.VMEM((nb, tm, d_ff), BF16)],
        compiler_params=_cparams(2),
        name="conv_ffn",
    )(h, h, h, modtab, *consts)


def _rope_tables(n_lat, n_ctx):
    t = jnp.arange(n_lat)
    row_id = (t // GRID_W).astype(F32)
    col_id = (t % GRID_W).astype(F32)
    inv = ROPE_BASE ** (-jnp.arange(ROPE_FREQS, dtype=F32) / ROPE_FREQS)
    d = np.arange(ATT_HEAD)
    freq = d % ROPE_FREQS
    ang = jnp.where((d // (2 * ROPE_FREQS) == 0)[None, :], row_id[:, None], col_id[:, None]) * inv[freq][None, :]
    sign = np.where((d // ROPE_FREQS) % 2 == 0, -1.0, 1.0).astype(np.float32)
    cos = jnp.concatenate([jnp.cos(ang), jnp.ones((n_ctx, ATT_HEAD), F32)], axis=0)
    sin = jnp.concatenate([jnp.sin(ang) * sign[None, :], jnp.zeros((n_ctx, ATT_HEAD), F32)], axis=0)
    return jnp.tile(cos, (1, ATT_HEADS)), jnp.tile(sin, (1, ATT_HEADS))


def _dft_mats(n):
    r = 1 << (int(math.log2(n)) // 2)
    u = jnp.arange(n, dtype=jnp.int32)[None, :]

    def table(t):
        ang = ((t[:, None] * u) % n).astype(F32) * (2.0 * math.pi / n)
        return jnp.cos(ang), jnp.sin(ang)

    c_hi, s_hi = table(jnp.arange(n // r, dtype=jnp.int32) * r)
    c_lo, s_lo = table(jnp.arange(r, dtype=jnp.int32))
    scale = 1.0 / math.sqrt(n)
    c_hi, s_hi = c_hi[:, None, :] * scale, s_hi[:, None, :] * scale
    cos = (c_hi * c_lo[None] - s_hi * s_lo[None]).reshape(n, n)
    sin = (s_hi * c_lo[None] + c_hi * s_lo[None]).reshape(n, n)
    return cos, sin


def _block_diag(m, groups):
    return jnp.kron(jnp.eye(groups, dtype=m.dtype), m)


def kernel(x, c, ctx, c_ctx, mod_w, mod_b, norm_mix_pre, norm_mix_post, norm_ffn_pre, norm_ffn_post, w_in, rwkv_mu, rwkv_w0, rwkv_w2, rwkv_a0, rwkv_a2, rwkv_g2, rwkv_k_k, rwkv_k_a, rwkv_r_k, rwkv_lnx_w, rwkv_lnx_b, rwkv_v0, rwkv_v1, rwkv_v2, attn_sink, w_branch_rwkv, w_branch_attn, w_branch_fourier, w_out, ffn_up, ffn_conv_w, ffn_conv_b, ffn_down):
    b, n_lat, d = x.shape
    n_ctx = ctx.shape[1]
    depth = mod_w.shape[0]
    d_ff = ffn_down.shape[1]
    l = n_lat + n_ctx
    tm = ROW_TILE
    assert n_lat % tm == 0 and n_ctx % tm == 0 and n_lat % n_ctx == 0
    assert n_lat >= 3 * ATT_BLOCK and d_ff % FF_CHUNK == 0 and b % WKV_BATCH == 0 and b % ROW_BATCH == 0
    n_lat_tiles = n_lat // tm

    cos_t, sin_t = _rope_tables(n_lat, n_ctx)
    cg, sg = _dft_mats(FOURIER_GROUP_DIM)
    dft_c = jnp.concatenate([_block_diag(cg, FOURIER_GROUPS), _block_diag(sg, FOURIER_GROUPS)], axis=1).astype(BF16)
    ct_lat, st_lat = (m.astype(BF16) for m in _dft_mats(n_lat))
    ct_ctx, st_ctx = (m.astype(BF16) for m in _dft_mats(n_ctx))
    seg = _block_diag(jnp.ones((RWKV_HEAD, RWKV_HEAD), F32), RWKV_HEADS).astype(BF16)
    avg = (seg.astype(F32) / RWKV_HEAD).astype(BF16)

    pad = (-(b + 1)) % 8
    cvec = jnp.concatenate([c, c_ctx[None, :], jnp.zeros((pad, d), F32)], axis=0)
    mod = _modulation(cvec, mod_w, mod_b)

    h = (x, ctx)
    v_first = None
    for layer in range(depth):
        last = layer == depth - 1
        lat = mod[layer, :b].reshape(b, 1, 6, d)
        cm = jnp.broadcast_to(mod[layer, b].reshape(1, 1, 6, d), (b, 1, 6, d))
        modtab = jnp.concatenate([lat, cm], axis=1)

        fp = {
            "mu": rwkv_mu[layer],
            "vec": jnp.stack([rwkv_k_k[layer], rwkv_k_a[layer], rwkv_r_k[layer]]),
            "w0": rwkv_w0[layer], "a0": rwkv_a0[layer],
            "w2": rwkv_w2[layer].astype(BF16), "a2": rwkv_a2[layer].astype(BF16),
            "g2": rwkv_g2[layer].astype(BF16), "seg": seg,
        }
        if layer > 0:
            lp = 128 - MV_LORA
            fp["v0"] = rwkv_v0[layer - 1].reshape(1, RWKV_DIM)
            fp["v1"] = jnp.pad(rwkv_v1[layer - 1], ((0, 0), (0, lp))).astype(BF16)
            fp["v2"] = jnp.pad(rwkv_v2[layer - 1], ((0, lp), (0, 0))).astype(BF16)
        q, kv, z, u_g, fc, ff, fb, ro = _inproj(h, modtab, norm_mix_pre[layer], w_in[layer], dft_c, cos_t, sin_t,
                                                v_first, fp, n_lat_tiles)
        if layer == 0:
            v_first = fc
        y_fwd, y_bwd = _wkv_scan(fc, ff, fb, n_lat)

        n_rows = n_lat if last else l
        y_att = _attention(q, kv, attn_sink[layer], n_lat, n_rows)
        y_dft = _token_dft(z, ct_lat, st_lat, n_lat, 0)
        if not last:
            y_dft = jnp.concatenate([y_dft, _token_dft(z, ct_ctx, st_ctx, n_ctx, n_lat // n_ctx)], axis=1)

        mp = {
            "ln": jnp.stack([rwkv_lnx_w[layer], rwkv_lnx_b[layer]]), "avg": avg,
            "wbr": w_branch_rwkv[layer].astype(BF16), "wba": w_branch_attn[layer].astype(BF16),
            "wbf": w_branch_fourier[layer].astype(BF16), "wo": w_out[layer].astype(BF16),
            "gpost": norm_mix_post[layer].reshape(1, d),
        }
        h = _merge(h, modtab, y_fwd, y_bwd, ro, y_att, y_dft, u_g, mp, n_rows, n_lat_tiles)

        up = ffn_up[layer].astype(BF16)
        pp = {
            "gpre": norm_ffn_pre[layer].reshape(1, d),
            "upg": up[:, :d_ff], "upv": up[:, d_ff:],
            "cw": jnp.concatenate([ffn_conv_w[layer], ffn_conv_b[layer][None, :]], axis=0),
            "dn": ffn_down[layer].astype(BF16),
            "gpost": norm_ffn_post[layer].reshape(1, d),
        }
        h = _ffn(h, modtab, pp, n_rows, n_lat_tiles)
    return h
```

```python
import functools
import math

import numpy as np
import jax
import jax.numpy as jnp
from jax import lax
from jax.experimental import pallas as pl
from jax.experimental.pallas import tpu as pltpu

F32 = jnp.float32
BF16 = jnp.bfloat16

GRID_W = 64
RWKV_HEADS = 4
RWKV_HEAD = 64
RWKV_DIM = RWKV_HEADS * RWKV_HEAD
DECAY_LORA = 64
AAA_LORA = 64
MV_LORA = 32
GATE_LORA = 128
LNX_EPS = 64e-5
ATT_HEADS = 8
ATT_KV_HEADS = 2
ATT_GROUP = ATT_HEADS // ATT_KV_HEADS
ATT_HEAD = 64
ATT_Q_DIM = ATT_HEADS * ATT_HEAD
ATT_KV_DIM = ATT_KV_HEADS * ATT_HEAD
ATT_SCALE = ATT_HEAD ** -0.5
V_BLOCK = 2 * ATT_HEAD
KV_COLS = ATT_KV_DIM + ATT_KV_HEADS * V_BLOCK
WINDOW = 128
ROPE_BASE = 10000.0
ROPE_FREQS = ATT_HEAD // 4
NEG_INF = -1e30
FOURIER_GROUPS = 4
FOURIER_GROUP_DIM = 64
FOURIER_DIM = FOURIER_GROUPS * FOURIER_GROUP_DIM
EPS = 1e-6
RWKV_COLS = 3 * RWKV_DIM + 2 * DECAY_LORA + 2 * AAA_LORA + GATE_LORA

ROW_TILE = 256
ROW_BATCH = 2
HALO = 8
WKV_CHUNK = 64
WKV_BATCH = 4
ATT_BLOCK = 256
FF_CHUNK = 256
VMEM_LIMIT = 56 * 1024 * 1024


def _cparams(n_axes):
    return pltpu.CompilerParams(dimension_semantics=("arbitrary",) * n_axes,
                                vmem_limit_bytes=VMEM_LIMIT)


def _dot(a, b):
    return jnp.dot(a.astype(BF16), b.astype(BF16), preferred_element_type=F32)


def _dot_nt(a, b):
    return lax.dot_general(a.astype(BF16), b.astype(BF16), (((1,), (1,)), ((), ())),
                           preferred_element_type=F32)


def _split3(x):
    hi = x.astype(BF16)
    r1 = x - hi.astype(F32)
    mid = r1.astype(BF16)
    lo = (r1 - mid.astype(F32)).astype(BF16)
    return hi, mid, lo


def _dot_exact_rhs(x, m):
    hi, mid, lo = _split3(x)
    return (jnp.dot(hi, m, preferred_element_type=F32) + jnp.dot(mid, m, preferred_element_type=F32)
            + jnp.dot(lo, m, preferred_element_type=F32))


def _dot_exact_lhs(m, x):
    hi, mid, lo = _split3(x)
    return (jnp.dot(m, hi, preferred_element_type=F32) + jnp.dot(m, mid, preferred_element_type=F32)
            + jnp.dot(m, lo, preferred_element_type=F32))


def _sigmoid(x):
    return 1.0 / (1.0 + jnp.exp(-x))


def _const_spec(shape):
    nd = len(shape)
    return pl.BlockSpec(shape, lambda *_: (0,) * nd, pipeline_mode=pl.Buffered(1))


def _tile_with_halo_specs(nb, tm, d, n_own, first):
    hb = tm // HALO
    own = lambda i: jnp.clip(i - first, 0, n_own - 1)
    return [pl.BlockSpec((nb, tm, d), lambda bb, i: (bb, own(i), 0)),
            pl.BlockSpec((nb, HALO, d), lambda bb, i: (bb, jnp.maximum(own(i) * hb - 1, 0), 0)),
            pl.BlockSpec((nb, HALO, d), lambda bb, i: (bb, jnp.minimum((own(i) + 1) * hb, n_own * hb - 1), 0))]


def _rms_mod(x, g, shift, scale):
    y = x * lax.rsqrt(jnp.mean(x * x, axis=-1, keepdims=True) + EPS) * g
    return y * (1.0 + scale) + shift


def _mod_kernel(c_ref, w_ref, b_ref, o_ref):
    x = c_ref[...]
    o_ref[0] = _dot(x * _sigmoid(x), w_ref[0]) + b_ref[0]


def _modulation(cvec, mod_w, mod_b):
    depth, d, n = mod_w.shape
    rows = cvec.shape[0]
    tn = 1536
    return pl.pallas_call(
        _mod_kernel,
        grid=(depth, n // tn),
        in_specs=[pl.BlockSpec((rows, d), lambda l, j: (0, 0)),
                  pl.BlockSpec((1, d, tn), lambda l, j: (l, 0, j)),
                  pl.BlockSpec((1, 1, tn), lambda l, j: (l, 0, j))],
        out_specs=pl.BlockSpec((1, rows, tn), lambda l, j: (l, 0, j)),
        out_shape=jax.ShapeDtypeStruct((depth, rows, n), F32),
        compiler_params=_cparams(2),
        name="adaln_modulation",
    )(cvec, mod_w.astype(BF16), mod_b.reshape(depth, 1, n))


def _rope(x, cos, sin_signed):
    n = x.shape[1]
    lane = lax.broadcasted_iota(jnp.int32, x.shape, 1)
    first = (lane & ROPE_FREQS) == 0
    partner = jnp.where(first, pltpu.roll(x, n - ROPE_FREQS, 1), pltpu.roll(x, ROPE_FREQS, 1))
    return x * cos + partner * sin_signed


def _inproj_kernel(*refs, split_input, has_vres, n_lat_tiles, n_tiles):
    i = pl.program_id(1)
    if split_input:
        is_lat = i < n_lat_tiles
        lat_refs, ctx_refs = refs[0:3], refs[3:6]
        tile_of = lambda bi, k: jnp.where(is_lat, lat_refs[k][bi], ctx_refs[k][bi])
        refs = refs[3:]
    else:
        lat_refs = refs[0:3]
        tile_of = lambda bi, k: lat_refs[k][bi]
    h_ref = refs[0]
    (mod_ref, g_ref, wr_ref, wq_ref, wkv_ref, wf_ref, wg_ref, dft_ref, cos_ref, sin_ref,
     mu_ref, vec_ref, w0_ref, a0_ref, w2_ref, a2_ref, g2_ref, seg_ref) = refs[3:21]
    refs = refs[21:]
    if has_vres:
        vf_ref, v0_ref, v1_ref, v2_ref = refs[:4]
        refs = refs[4:]
    q_ref, kv_ref, z_ref, ug_ref, fc_ref, ff_ref, fb_ref, ro_ref, buf_ref = refs
    nb, tm, _ = h_ref.shape
    th = tm + 2 * HALO
    prev_ok = jnp.logical_and(i != 0, i != n_lat_tiles)
    next_ok = jnp.logical_and(i != n_lat_tiles - 1, i != n_tiles - 1)
    rid = lax.broadcasted_iota(jnp.int32, (th, 1), 0)
    live = jnp.logical_and(jnp.logical_or(rid >= HALO, prev_ok), jnp.logical_or(rid < HALO + tm, next_ok))
    a_all, a_main = [], []
    for bi in range(nb):
        x = jnp.concatenate([tile_of(bi, 1), tile_of(bi, 0), tile_of(bi, 2)], axis=0)
        a = _rms_mod(x, g_ref[...], mod_ref[bi, 0, 0:1, :], mod_ref[bi, 0, 1:2, :])
        a_all.append(jnp.where(live, a, 0.0).astype(BF16))
        a_main.append(a[HALO:HALO + tm].astype(BF16))
    a_all = jnp.concatenate(a_all, axis=0)
    a = jnp.concatenate(a_main, axis=0)
    cos = jnp.concatenate([cos_ref[...]] * nb, axis=0)
    sin = jnp.concatenate([sin_ref[...]] * nb, axis=0)
    buf_ref[...] = jnp.dot(a_all, wr_ref[...], preferred_element_type=F32).reshape(nb, th, -1)
    q = jnp.dot(a, wq_ref[...], preferred_element_type=F32)
    q_ref[...] = (_rope(q, cos, sin) * ATT_SCALE).astype(BF16).reshape(nb, tm, -1)
    kv = jnp.dot(a, wkv_ref[...], preferred_element_type=F32)
    k = _rope(kv[:, :ATT_KV_DIM], cos[:, :ATT_KV_DIM], sin[:, :ATT_KV_DIM])
    kv_ref[:, :, :ATT_KV_DIM] = k.astype(BF16).reshape(nb, tm, -1)
    v = kv[:, ATT_KV_DIM:]
    low = lax.broadcasted_iota(jnp.int32, v.shape, 1) < ATT_HEAD
    for g in range(ATT_KV_HEADS):
        v_g = v if g == 0 else pltpu.roll(v, ATT_KV_DIM - g * ATT_HEAD, 1)
        kv_ref[:, :, ATT_KV_DIM + g * V_BLOCK:ATT_KV_DIM + (g + 1) * V_BLOCK] = (
            jnp.where(low, v_g, 1.0)[:, :V_BLOCK].astype(BF16).reshape(nb, tm, -1))
    uf = jnp.dot(a, wf_ref[...], preferred_element_type=F32)
    z_ref[...] = _dot(uf, dft_ref[...]).astype(BF16).reshape(nb, tm, -1)
    half = wg_ref.shape[1] // 2
    ug_ref[:, :, :half] = jnp.dot(a, wg_ref[:, :half], preferred_element_type=F32).astype(BF16).reshape(nb, tm, -1)
    _rwkv_features(buf_ref, tm, vf_ref if has_vres else None, (v0_ref, v1_ref, v2_ref) if has_vres else None,
                   mu_ref, vec_ref, w0_ref, a0_ref, w2_ref, a2_ref, g2_ref, seg_ref, fc_ref, ff_ref, fb_ref, ro_ref)
    ug_ref[:, :, half:] = jnp.dot(a, wg_ref[:, half:], preferred_element_type=F32).astype(BF16).reshape(nb, tm, -1)


def _inproj(h, modtab, g, w_in, dft_c, cos_t, sin_t, v_first_src, p, n_lat_tiles):
    split_input = isinstance(h, tuple)
    parts = h if split_input else (h,)
    b, _, d = parts[0].shape
    l = sum(x.shape[1] for x in parts)
    tm = ROW_TILE
    n = RWKV_DIM
    n_tiles = l // tm
    hb = tm // HALO
    n_g = w_in.shape[1] - (RWKV_COLS + ATT_Q_DIM + 2 * ATT_KV_DIM + FOURIER_DIM)
    o = np.cumsum([0, RWKV_COLS, ATT_Q_DIM, 2 * ATT_KV_DIM, FOURIER_DIM, n_g])
    wb = w_in.astype(BF16)
    ws = [wb[:, o[i]:o[i + 1]] for i in range(5)]
    nb = ROW_BATCH
    has_vres = v_first_src is not None
    row = lambda bb, i: (bb, i, 0)
    consts = [g.reshape(1, d)] + ws + [dft_c]
    feat_consts = [p["mu"], p["vec"], p["w0"], p["a0"], p["w2"], p["a2"], p["g2"], p["seg"]]
    in_specs, args, first = [], [], 0
    for x in parts:
        in_specs += _tile_with_halo_specs(nb, tm, d, x.shape[1] // tm, first)
        args += [x, x, x]
        first += x.shape[1] // tm
    in_specs += ([pl.BlockSpec((nb, 1, 6, d), lambda bb, i: (bb, jnp.where(i < n_lat_tiles, 0, 1), 0, 0))]
                 + [_const_spec(x.shape) for x in consts]
                 + [pl.BlockSpec((tm, ATT_Q_DIM), lambda bb, i: (i, 0)),
                    pl.BlockSpec((tm, ATT_Q_DIM), lambda bb, i: (i, 0))]
                 + [_const_spec(x.shape) for x in feat_consts])
    args += [modtab] + consts + [cos_t, sin_t] + feat_consts
    if has_vres:
        vres_consts = [p["v0"], p["v1"], p["v2"]]
        in_specs += [pl.BlockSpec((nb, tm, n), lambda bb, i: (bb, i, 1))] + [_const_spec(x.shape) for x in vres_consts]
        args += [v_first_src] + vres_consts
    outs = [(ATT_Q_DIM, BF16), (KV_COLS, BF16), (2 * FOURIER_DIM, BF16), (n_g, BF16),
            (3 * n, F32), (3 * n, F32), (3 * n, F32), (2 * n, F32)]
    return pl.pallas_call(
        functools.partial(_inproj_kernel, split_input=split_input, has_vres=has_vres, n_lat_tiles=n_lat_tiles,
                          n_tiles=n_tiles),
        grid=(b // nb, n_tiles),
        in_specs=in_specs,
        out_specs=[pl.BlockSpec((nb, tm, w), row) for w, _ in outs],
        out_shape=[jax.ShapeDtypeStruct((b, l, w), dt) for w, dt in outs],
        scratch_shapes=[pltpu.VMEM((nb, tm + 2 * HALO, RWKV_COLS), F32)],
        compiler_params=_cparams(2),
        name="in_projection",
    )(*args)


def _rwkv_features(buf_ref, tm, vf_ref, vres, mu_ref, vec_ref, w0_ref, a0_ref, w2_ref, a2_ref, g2_ref, seg_ref,
                   fc_ref, ff_ref, fb_ref, ro_ref):
    n = RWKV_DIM
    k_k = vec_ref[0:1, :]
    k_a = vec_ref[1:2, :]
    r_k = vec_ref[2:3, :]
    seg = seg_ref[...]
    tiles = []
    for bi in range(buf_ref.shape[0]):
        u = buf_ref[bi, HALO:HALO + tm, :]
        u_prev = buf_ref[bi, HALO - 1:HALO - 1 + tm, :]
        u_next = buf_ref[bi, HALO + 1:HALO + 1 + tm, :]
        us = u + mu_ref[0:1, :] * (u_prev - u) + mu_ref[1:2, :] * (u_next - u)
        o = 3 * n
        t = dict(bi=bi, r=us[:, 0:n], k=us[:, n:2 * n], v=us[:, 2 * n:3 * n])
        t["tw"] = [jnp.tanh(us[:, o + d * DECAY_LORA:o + (d + 1) * DECAY_LORA]).astype(BF16) for d in range(2)]
        o += 2 * DECAY_LORA
        t["xa"] = [us[:, o + d * AAA_LORA:o + (d + 1) * AAA_LORA].astype(BF16) for d in range(2)]
        o += 2 * AAA_LORA
        t["sg"] = _sigmoid(us[:, o:o + GATE_LORA]).astype(BF16)
        t["kk"] = t["k"] * k_k
        t["kk_sq"] = _split3(t["kk"] * t["kk"])
        tiles.append(t)
    for t in tiles:
        t["z"] = [jnp.dot(t["tw"][d], w2_ref[d], preferred_element_type=F32) for d in range(2)]
        t["za"] = [jnp.dot(t["xa"][d], a2_ref[d], preferred_element_type=F32) for d in range(2)]
        t["g"] = jnp.dot(t["sg"], g2_ref[...], preferred_element_type=F32)
        t["ss"] = sum(jnp.dot(part, seg, preferred_element_type=F32) for part in t["kk_sq"])
        if vres is not None:
            t["vv"] = _dot(t["v"], vres[1][...])
    if vres is not None:
        for t in tiles:
            t["vg"] = _dot(t["vv"], vres[2][...])
    for t in tiles:
        bi = t["bi"]
        v = t["v"]
        if vres is not None:
            v = v + (vf_ref[bi] - v) * _sigmoid(vres[0][...] + t["vg"])
        kk = t["kk"] * lax.rsqrt(jnp.maximum(t["ss"], 1e-24))
        keys = []
        for d, out_ref in enumerate((ff_ref, fb_ref)):
            z = w0_ref[d:d + 1, :] + t["z"][d]
            softplus = jnp.maximum(-z, 0.0) + jnp.log(1.0 + jnp.exp(-jnp.abs(z)))
            out_ref[bi, :, 0:n] = -jnp.exp(-softplus - 0.5)
            a = _sigmoid(a0_ref[d:d + 1, :] + t["za"][d])
            key = t["k"] * (1.0 + (a - 1.0) * k_a)
            keys.append(key)
            out_ref[bi, :, n:2 * n] = key
            out_ref[bi, :, 2 * n:3 * n] = kk * a
        t["v"] = v
        t["bonus_in"] = _split3(t["r"] * (0.5 * (keys[0] + keys[1])) * r_k)
        fc_ref[bi, :, 0:n] = t["r"]
        fc_ref[bi, :, n:2 * n] = v
        fc_ref[bi, :, 2 * n:3 * n] = kk
        ro_ref[bi, :, n:2 * n] = t["g"]
    for t in tiles:
        bonus = sum(jnp.dot(part, seg, preferred_element_type=F32) for part in t["bonus_in"])
        ro_ref[t["bi"], :, 0:n] = bonus * t["v"]


def _wkv_masks(c, reverse):
    ti = lax.broadcasted_iota(jnp.int32, (c, c), 0)
    si = lax.broadcasted_iota(jnp.int32, (c, c), 1)
    incl, strict = (si >= ti, si > ti) if reverse else (si <= ti, si < ti)
    levels = []
    for sh in range(int(math.log2(c))):
        bt = lax.shift_right_logical(ti, sh)
        bs = lax.shift_right_logical(si, sh)
        if reverse:
            levels.append(jnp.logical_and((bt & 1) == 0, bs == bt + 1))
        else:
            levels.append(jnp.logical_and((bt & 1) == 1, bs == bt - 1))
    return incl, strict, levels, jnp.where(ti == si, 1.0, 0.0)


def _wkv_kernel(fcf_ref, ff_ref, fcb_ref, fb_ref, yf_ref, yb_ref, sf_ref, sb_ref):
    @pl.when(pl.program_id(1) == 0)
    def _():
        sf_ref[...] = jnp.zeros_like(sf_ref)
        sb_ref[...] = jnp.zeros_like(sb_ref)

    c = WKV_CHUNK
    n = RWKV_DIM
    hd = RWKV_HEAD
    nb = fcf_ref.shape[0]
    probs = []
    for fc_ref, fd_ref, y_ref, s_ref, reverse in ((fcf_ref, ff_ref, yf_ref, sf_ref, False),
                                                  (fcb_ref, fb_ref, yb_ref, sb_ref, True)):
        incl, strict, levels, eye = _wkv_masks(c, reverse)
        tri = jnp.where(incl, 1.0, 0.0).astype(BF16)
        cums = [_dot_exact_lhs(tri, fd_ref[bi, :, 0:n]) for bi in range(nb)]
        for bi in range(nb):
            cum = cums[bi]
            total = cum[0:1, :] if reverse else cum[c - 1:c, :]
            c0 = 0.5 * total
            e_neg = jnp.exp(c0 - cum)
            a_t = -fc_ref[bi, :, 2 * n:3 * n] * jnp.exp(cum - fd_ref[bi, :, 0:n] - c0)
            r_t = fc_ref[bi, :, 0:n] * jnp.exp(cum - c0)
            b_t = fd_ref[bi, :, 2 * n:3 * n] * e_neg
            k_t = fd_ref[bi, :, n:2 * n] * e_neg
            e_half = jnp.exp(c0)
            e_tot = jnp.exp(total)
            for h in range(RWKV_HEADS):
                sl = slice(h * hd, (h + 1) * hd)
                probs.append(dict(
                    lhs=jnp.concatenate([a_t[:, sl], r_t[:, sl]], axis=0).astype(BF16),
                    rhs=jnp.concatenate([b_t[:, sl], k_t[:, sl]], axis=0).astype(BF16),
                    v=fc_ref[bi, :, n + h * hd:n + (h + 1) * hd], s0=s_ref[bi, h],
                    e_half=e_half[:, sl], e_tot=e_tot[:, sl], incl=incl, strict=strict, levels=levels, eye=eye,
                    y_ref=y_ref, s_ref=s_ref, bi=bi, h=h, sl=sl))

    for p in probs:
        p["g"] = _dot_nt(p["lhs"], p["rhs"])
    for p in probs:
        p["a_s"] = _dot_nt(p["lhs"], p["s0"] * p["e_half"])
    for p in probs:
        g = p["g"]
        p["a_ab"] = jnp.where(p["strict"], g[:c, :c], 0.0)
        a_ak = jnp.where(p["strict"], g[:c, c:], 0.0)
        p["a_r"] = jnp.concatenate([jnp.where(p["incl"], g[c:, :c], 0.0),
                                    jnp.where(p["incl"], g[c:, c:], 0.0)], axis=1).astype(BF16)
        p["t"] = p["eye"] + jnp.where(p["levels"][0], p["a_ab"], 0.0)
        p["rhs_u"] = p["a_s"][:c] + _dot(a_ak, p["v"])
    for lvl in range(1, len(probs[0]["levels"])):
        for p in probs:
            p["tb"] = p["t"].astype(BF16)
            p["tmp"] = _dot(jnp.where(p["levels"][lvl], p["a_ab"], 0.0), p["tb"])
        for p in probs:
            p["t"] = p["t"] + _dot(p["tb"], p["tmp"])
    for p in probs:
        p["uv"] = jnp.concatenate([_dot(p["t"], p["rhs_u"]), p["v"]], axis=0).astype(BF16)
    for p in probs:
        p["y_ref"][p["bi"], :, p["sl"]] = p["a_s"][c:] + _dot(p["a_r"], p["uv"])
    for p in probs:
        upd = lax.dot_general(p["uv"], p["rhs"], (((0,), (0,)), ((), ())), preferred_element_type=F32)
        p["s_ref"][p["bi"], p["h"]] = p["s0"] * p["e_tot"] + upd * p["e_half"]


def _wkv_scan(fc, ff, fb, n_lat):
    b, l, w = fc.shape
    c = WKV_CHUNK
    nb = WKV_BATCH
    nl = n_lat // c
    nc = l // c - nl
    fwd = lambda bb, j: (bb, jnp.where(j < nc, nl + j, j - nc), 0)
    bwd = lambda bb, j: (bb, nl + nc - 1 - j, 0)
    return pl.pallas_call(
        _wkv_kernel,
        grid=(b // nb, nl + nc),
        in_specs=[pl.BlockSpec((nb, c, w), fwd), pl.BlockSpec((nb, c, w), fwd),
                  pl.BlockSpec((nb, c, w), bwd), pl.BlockSpec((nb, c, w), bwd)],
        out_specs=[pl.BlockSpec((nb, c, RWKV_DIM), fwd), pl.BlockSpec((nb, c, RWKV_DIM), bwd)],
        out_shape=[jax.ShapeDtypeStruct((b, l, RWKV_DIM), F32)] * 2,
        scratch_shapes=[pltpu.VMEM((nb, RWKV_HEADS, RWKV_HEAD, RWKV_HEAD), F32)] * 2,
        compiler_params=_cparams(2),
        name="wkv_scan",
    )(fc, ff, fc, fb)


def _attn_kernel(sink_ref, q_ref, kv_ref, o_ref, *, n_lat):
    j = pl.program_id(1)
    qb = ATT_BLOCK
    hd = ATT_HEAD
    n_ctx = kv_ref.shape[1] - n_lat
    n_win = qb + 2 * WINDOW
    is_lat = j * qb < n_lat
    ws = pl.multiple_of(jnp.clip(j * qb - WINDOW, 0, n_lat - n_win), WINDOW)
    col = lax.broadcasted_iota(jnp.int32, (qb, n_ctx + n_win), 1)
    q_pos = j * qb + lax.broadcasted_iota(jnp.int32, (qb, n_ctx + n_win), 0)
    k_pos = ws + col - n_ctx
    valid = jnp.logical_or(col < n_ctx, jnp.logical_and(jnp.abs(q_pos - k_pos) <= WINDOW, is_lat))
    q = q_ref[0]
    kv = jnp.concatenate([kv_ref[0, n_lat:n_lat + n_ctx, :], kv_ref[0, pl.ds(ws, n_win), :]], axis=0)
    heads = range(ATT_HEADS)
    s = [jnp.where(valid, _dot_nt(q[:, h * hd:(h + 1) * hd],
                                  kv[:, (h // ATT_GROUP) * hd:(h // ATT_GROUP + 1) * hd]), NEG_INF) for h in heads]
    m = [jnp.maximum(jnp.max(s[h], axis=-1, keepdims=True), sink_ref[h]) for h in heads]
    p = [jnp.exp((s[h] - m[h]).astype(BF16)) for h in heads]
    o = [jnp.dot(p[h], kv[:, ATT_KV_DIM + (h // ATT_GROUP) * V_BLOCK:ATT_KV_DIM + (h // ATT_GROUP + 1) * V_BLOCK],
                 preferred_element_type=F32) for h in heads]
    for h in heads:
        den = pltpu.roll(o[h], hd, 1) + jnp.exp(sink_ref[h] - m[h])
        o_ref[0, :, h * hd:(h + 1) * hd] = (o[h] / den)[:, :hd].astype(o_ref.dtype)


def _attention(q, kv, sink, n_lat, n_rows):
    b, l, _ = q.shape
    qb = ATT_BLOCK
    return pl.pallas_call(
        functools.partial(_attn_kernel, n_lat=n_lat),
        grid=(b, n_rows // qb),
        in_specs=[pl.BlockSpec(memory_space=pltpu.SMEM),
                  pl.BlockSpec((1, qb, ATT_Q_DIM), lambda bb, j: (bb, j, 0)),
                  pl.BlockSpec((1, l, kv.shape[2]), lambda bb, j: (bb, 0, 0))],
        out_specs=pl.BlockSpec((1, qb, ATT_Q_DIM), lambda bb, j: (bb, j, 0)),
        out_shape=jax.ShapeDtypeStruct((b, n_rows, ATT_Q_DIM), BF16),
        compiler_params=_cparams(2),
        name="windowed_attention",
    )(sink, q, kv)


def _dft_kernel(z_ref, ct_ref, st_ref, o_ref):
    n = FOURIER_DIM
    o_ref[0] = (jnp.dot(ct_ref[...], z_ref[0, :, 0:n], preferred_element_type=F32)
                - jnp.dot(st_ref[...], z_ref[0, :, n:2 * n], preferred_element_type=F32)).astype(o_ref.dtype)


def _token_dft(z, ct, st, seg_rows, seg_block):
    b = z.shape[0]
    return pl.pallas_call(
        _dft_kernel,
        grid=(b,),
        in_specs=[pl.BlockSpec((1, seg_rows, z.shape[2]), lambda bb: (bb, seg_block, 0)),
                  _const_spec(ct.shape), _const_spec(st.shape)],
        out_specs=pl.BlockSpec((1, seg_rows, FOURIER_DIM), lambda bb: (bb, 0, 0)),
        out_shape=jax.ShapeDtypeStruct((b, seg_rows, FOURIER_DIM), BF16),
        compiler_params=_cparams(1),
        name="token_dft",
    )(z, ct, st)


def _merge_kernel(*refs, split_input, n_lat_tiles):
    if split_input:
        is_lat = pl.program_id(1) < n_lat_tiles
        x_ref, c_ref = refs[0:2]
        residual = lambda bi: jnp.where(is_lat, x_ref[bi], c_ref[bi])
        refs = refs[1:]
    else:
        residual = lambda bi: h_ref[bi]
    (h_ref, mod_ref, yf_ref, yb_ref, ro_ref, ya_ref, yd_ref, ug_ref, ln_ref, avg_ref,
     wbr_ref, wba_ref, wbf_ref, wo_ref, gp_ref, o_ref) = refs
    n = RWKV_DIM
    d = h_ref.shape[2]
    avg = avg_ref[...]
    nb, tm, _ = h_ref.shape
    rows = nb * tm
    y = (yf_ref[...] + yb_ref[...]).reshape(rows, n)
    mean = _dot_exact_rhs(y, avg)
    mix = (_sigmoid(ug_ref[:, :, d:2 * d].reshape(rows, d).astype(F32))
           * jnp.dot(ya_ref[...].reshape(rows, -1), wba_ref[...], preferred_element_type=F32))
    dev = y - mean
    var = _dot_exact_rhs(dev * dev, avg)
    mix += (_sigmoid(ug_ref[:, :, 2 * d:3 * d].reshape(rows, d).astype(F32))
            * jnp.dot(yd_ref[...].reshape(rows, -1), wbf_ref[...], preferred_element_type=F32))
    yn = dev * lax.rsqrt(var + LNX_EPS) * ln_ref[0:1, :] + ln_ref[1:2, :]
    ro = ro_ref[...].reshape(rows, 2 * n)
    y_r = (yn + ro[:, 0:n]) * ro[:, n:2 * n]
    mix += _sigmoid(ug_ref[:, :, 0:d].reshape(rows, d).astype(F32)) * _dot(y_r, wbr_ref[...])
    o = _dot(mix, wo_ref[...])
    o = o * lax.rsqrt(jnp.mean(o * o, axis=-1, keepdims=True) + EPS) * gp_ref[...]
    for bi in range(nb):
        o_ref[bi] = residual(bi) + mod_ref[bi, 0, 2:3, :] * o[bi * tm:(bi + 1) * tm]


def _merge(h, modtab, yf, yb, ro, ya, yd, ug, p, n_rows, n_lat_tiles):
    split_input = isinstance(h, tuple)
    parts = h if split_input else (h,)
    b, _, d = parts[0].shape
    tm = ROW_TILE
    nb = ROW_BATCH
    row = lambda bb, i: (bb, i, 0)
    consts = [p["ln"], p["avg"], p["wbr"], p["wba"], p["wbf"], p["wo"], p["gpost"]]
    h_specs, first = [], 0
    for x in parts:
        h_specs.append(_tile_with_halo_specs(nb, tm, d, x.shape[1] // tm, first)[0])
        first += x.shape[1] // tm
    return pl.pallas_call(
        functools.partial(_merge_kernel, split_input=split_input, n_lat_tiles=n_lat_tiles),
        grid=(b // nb, n_rows // tm),
        in_specs=h_specs + [
                  pl.BlockSpec((nb, 1, 6, d), lambda bb, i: (bb, jnp.where(i < n_lat_tiles, 0, 1), 0, 0)),
                  pl.BlockSpec((nb, tm, RWKV_DIM), row), pl.BlockSpec((nb, tm, RWKV_DIM), row),
                  pl.BlockSpec((nb, tm, 2 * RWKV_DIM), row),
                  pl.BlockSpec((nb, tm, ATT_Q_DIM), row), pl.BlockSpec((nb, tm, FOURIER_DIM), row),
                  pl.BlockSpec((nb, tm, 3 * d), row)] + [_const_spec(x.shape) for x in consts],
        out_specs=pl.BlockSpec((nb, tm, d), row),
        out_shape=jax.ShapeDtypeStruct((b, n_rows, d), F32),
        compiler_params=_cparams(2),
        name="branch_merge",
    )(*parts, modtab, yf, yb, ro, ya, yd, ug, *consts)


def _ffn_kernel(h_ref, hp_ref, hn_ref, mod_ref, gpre_ref, upg_ref, upv_ref, cw_ref, dn_ref, gpost_ref, o_ref,
                zg_ref, act_ref, *, n_lat_tiles, n_tiles):
    i = pl.program_id(1)
    tm = h_ref.shape[1]
    prev_ok = jnp.logical_and(i != 0, i != n_lat_tiles)
    next_ok = jnp.logical_and(i != n_lat_tiles - 1, i != n_tiles - 1)
    rid = lax.broadcasted_iota(jnp.int32, (tm + 2 * HALO, 1), 0)
    live = jnp.logical_and(jnp.logical_or(rid >= HALO, prev_ok), jnp.logical_or(rid < HALO + tm, next_ok))
    nb = h_ref.shape[0]
    th = tm + 2 * HALO
    f_all, f_main = [], []
    for bi in range(nb):
        x = jnp.concatenate([hp_ref[bi], h_ref[bi], hn_ref[bi]], axis=0)
        f = _rms_mod(x, gpre_ref[...], mod_ref[bi, 0, 3:4, :], mod_ref[bi, 0, 4:5, :])
        f_all.append(jnp.where(live, f, 0.0).astype(BF16))
        f_main.append(f[HALO:HALO + tm].astype(BF16))
    f_all = jnp.concatenate(f_all, axis=0)
    f_main = jnp.concatenate(f_main, axis=0)
    for c in range(zg_ref.shape[2] // FF_CHUNK):
        cols = slice(c * FF_CHUNK, (c + 1) * FF_CHUNK)
        zg_ref[:, :, cols] = jnp.dot(f_all, upg_ref[:, cols], preferred_element_type=F32).reshape(nb, th, -1)
        zv = jnp.dot(f_main, upv_ref[:, cols], preferred_element_type=F32).reshape(nb, tm, -1)
        zg = (cw_ref[0:1, cols] * zg_ref[:, HALO - 1:HALO - 1 + tm, cols]
              + cw_ref[1:2, cols] * zg_ref[:, HALO:HALO + tm, cols]
              + cw_ref[2:3, cols] * zg_ref[:, HALO + 1:HALO + 1 + tm, cols] + cw_ref[3:4, cols])
        act = 0.5 * zg * (1.0 + jnp.tanh(0.7978845608028654 * (zg + 0.044715 * zg * zg * zg)))
        act_ref[:, :, cols] = (act * zv).astype(BF16)
    o = jnp.dot(act_ref[...].reshape(nb * tm, -1), dn_ref[...], preferred_element_type=F32)
    o = o * lax.rsqrt(jnp.mean(o * o, axis=-1, keepdims=True) + EPS) * gpost_ref[...]
    for bi in range(nb):
        o_ref[bi] = h_ref[bi] + mod_ref[bi, 0, 5:6, :] * o[bi * tm:(bi + 1) * tm]


def _ffn(h, modtab, p, n_rows, n_lat_tiles):
    b, l, d = h.shape
    tm = ROW_TILE
    hb = tm // HALO
    n_tiles = l // tm
    d_ff = p["dn"].shape[0]
    nb = ROW_BATCH
    row = lambda bb, i: (bb, i, 0)
    consts = [p["gpre"], p["upg"], p["upv"], p["cw"], p["dn"], p["gpost"]]
    return pl.pallas_call(
        functools.partial(_ffn_kernel, n_lat_tiles=n_lat_tiles, n_tiles=n_tiles),
        grid=(b // nb, n_rows // tm),
        in_specs=[pl.BlockSpec((nb, tm, d), row),
                  pl.BlockSpec((nb, HALO, d), lambda bb, i: (bb, jnp.maximum(i * hb - 1, 0), 0)),
                  pl.BlockSpec((nb, HALO, d), lambda bb, i: (bb, jnp.minimum((i + 1) * hb, l // HALO - 1), 0)),
                  pl.BlockSpec((nb, 1, 6, d), lambda bb, i: (bb, jnp.where(i < n_lat_tiles, 0, 1), 0, 0))]
        + [_const_spec(x.shape) for x in consts],
        out_specs=pl.BlockSpec((nb, tm, d), row),
        out_shape=jax.ShapeDtypeStruct((b, n_rows, d), F32),
        scratch_shapes=[pltpu.VMEM((nb, tm + 2 * HALO, d_ff), F32), pltpu.VMEM((nb, tm, d_ff), BF16)],
        compiler_params=_cparams(2),
        name="conv_ffn",
    )(h, h, h, modtab, *consts)


def _rope_tables(n_lat, n_ctx):
    t = jnp.arange(n_lat)
    row_id = (t // GRID_W).astype(F32)
    col_id = (t % GRID_W).astype(F32)
    inv = ROPE_BASE ** (-jnp.arange(ROPE_FREQS, dtype=F32) / ROPE_FREQS)
    d = np.arange(ATT_HEAD)
    freq = d % ROPE_FREQS
    ang = jnp.where((d // (2 * ROPE_FREQS) == 0)[None, :], row_id[:, None], col_id[:, None]) * inv[freq][None, :]
    sign = np.where((d // ROPE_FREQS) % 2 == 0, -1.0, 1.0).astype(np.float32)
    cos = jnp.concatenate([jnp.cos(ang), jnp.ones((n_ctx, ATT_HEAD), F32)], axis=0)
    sin = jnp.concatenate([jnp.sin(ang) * sign[None, :], jnp.zeros((n_ctx, ATT_HEAD), F32)], axis=0)
    return jnp.tile(cos, (1, ATT_HEADS)), jnp.tile(sin, (1, ATT_HEADS))


def _dft_mats(n):
    r = 1 << (int(math.log2(n)) // 2)
    u = jnp.arange(n, dtype=jnp.int32)[None, :]

    def table(t):
        ang = ((t[:, None] * u) % n).astype(F32) * (2.0 * math.pi / n)
        return jnp.cos(ang), jnp.sin(ang)

    c_hi, s_hi = table(jnp.arange(n // r, dtype=jnp.int32) * r)
    c_lo, s_lo = table(jnp.arange(r, dtype=jnp.int32))
    scale = 1.0 / math.sqrt(n)
    c_hi, s_hi = c_hi[:, None, :] * scale, s_hi[:, None, :] * scale
    cos = (c_hi * c_lo[None] - s_hi * s_lo[None]).reshape(n, n)
    sin = (s_hi * c_lo[None] + c_hi * s_lo[None]).reshape(n, n)
    return cos, sin


def _block_diag(m, groups):
    return jnp.kron(jnp.eye(groups, dtype=m.dtype), m)


def kernel(x, c, ctx, c_ctx, mod_w, mod_b, norm_mix_pre, norm_mix_post, norm_ffn_pre, norm_ffn_post, w_in, rwkv_mu, rwkv_w0, rwkv_w2, rwkv_a0, rwkv_a2, rwkv_g2, rwkv_k_k, rwkv_k_a, rwkv_r_k, rwkv_lnx_w, rwkv_lnx_b, rwkv_v0, rwkv_v1, rwkv_v2, attn_sink, w_branch_rwkv, w_branch_attn, w_branch_fourier, w_out, ffn_up, ffn_conv_w, ffn_conv_b, ffn_down):
    b, n_lat, d = x.shape
    n_ctx = ctx.shape[1]
    depth = mod_w.shape[0]
    d_ff = ffn_down.shape[1]
    l = n_lat + n_ctx
    tm = ROW_TILE
    assert n_lat % tm == 0 and n_ctx % tm == 0 and n_lat % n_ctx == 0
    assert n_lat >= ATT_BLOCK + 2 * WINDOW and n_lat % ATT_BLOCK == 0 and n_ctx % ATT_BLOCK == 0
    assert ATT_BLOCK % WINDOW == 0 and d_ff % FF_CHUNK == 0 and b % WKV_BATCH == 0 and b % ROW_BATCH == 0
    n_lat_tiles = n_lat // tm

    cos_t, sin_t = _rope_tables(n_lat, n_ctx)
    cg, sg = _dft_mats(FOURIER_GROUP_DIM)
    dft_c = jnp.concatenate([_block_diag(cg, FOURIER_GROUPS), _block_diag(sg, FOURIER_GROUPS)], axis=1).astype(BF16)
    ct_lat, st_lat = (m.astype(BF16) for m in _dft_mats(n_lat))
    ct_ctx, st_ctx = (m.astype(BF16) for m in _dft_mats(n_ctx))
    seg = _block_diag(jnp.ones((RWKV_HEAD, RWKV_HEAD), F32), RWKV_HEADS).astype(BF16)
    avg = (seg.astype(F32) / RWKV_HEAD).astype(BF16)

    pad = (-(b + 1)) % 8
    cvec = jnp.concatenate([c, c_ctx[None, :], jnp.zeros((pad, d), F32)], axis=0)
    mod = _modulation(cvec, mod_w, mod_b)

    h = (x, ctx)
    v_first = None
    for layer in range(depth):
        last = layer == depth - 1
        lat = mod[layer, :b].reshape(b, 1, 6, d)
        cm = jnp.broadcast_to(mod[layer, b].reshape(1, 1, 6, d), (b, 1, 6, d))
        modtab = jnp.concatenate([lat, cm], axis=1)

        fp = {
            "mu": rwkv_mu[layer],
            "vec": jnp.stack([rwkv_k_k[layer], rwkv_k_a[layer], rwkv_r_k[layer]]),
            "w0": rwkv_w0[layer], "a0": rwkv_a0[layer],
            "w2": rwkv_w2[layer].astype(BF16), "a2": rwkv_a2[layer].astype(BF16),
            "g2": rwkv_g2[layer].astype(BF16), "seg": seg,
        }
        if layer > 0:
            lp = 128 - MV_LORA
            fp["v0"] = rwkv_v0[layer - 1].reshape(1, RWKV_DIM)
            fp["v1"] = jnp.pad(rwkv_v1[layer - 1], ((0, 0), (0, lp))).astype(BF16)
            fp["v2"] = jnp.pad(rwkv_v2[layer - 1], ((0, lp), (0, 0))).astype(BF16)
        q, kv, z, u_g, fc, ff, fb, ro = _inproj(h, modtab, norm_mix_pre[layer], w_in[layer], dft_c, cos_t, sin_t,
                                                v_first, fp, n_lat_tiles)
        if layer == 0:
            v_first = fc
        y_fwd, y_bwd = _wkv_scan(fc, ff, fb, n_lat)

        n_rows = n_lat if last else l
        y_att = _attention(q, kv, attn_sink[layer], n_lat, n_rows)
        y_dft = _token_dft(z, ct_lat, st_lat, n_lat, 0)
        if not last:
            y_dft = jnp.concatenate([y_dft, _token_dft(z, ct_ctx, st_ctx, n_ctx, n_lat // n_ctx)], axis=1)

        mp = {
            "ln": jnp.stack([rwkv_lnx_w[layer], rwkv_lnx_b[layer]]), "avg": avg,
            "wbr": w_branch_rwkv[layer].astype(BF16), "wba": w_branch_attn[layer].astype(BF16),
            "wbf": w_branch_fourier[layer].astype(BF16), "wo": w_out[layer].astype(BF16),
            "gpost": norm_mix_post[layer].reshape(1, d),
        }
        h = _merge(h, modtab, y_fwd, y_bwd, ro, y_att, y_dft, u_g, mp, n_rows, n_lat_tiles)

        up = ffn_up[layer].astype(BF16)
        pp = {
            "gpre": norm_ffn_pre[layer].reshape(1, d),
            "upg": up[:, :d_ff], "upv": up[:, d_ff:],
            "cw": jnp.concatenate([ffn_conv_w[layer], ffn_conv_b[layer][None, :]], axis=0),
            "dn": ffn_down[layer].astype(BF16),
            "gpost": norm_ffn_post[layer].reshape(1, d),
        }
        h = _ffn(h, modtab, pp, n_rows, n_lat_tiles)
    return h
```

```python
import functools
import math

import numpy as np
import jax
import jax.numpy as jnp
from jax import lax
from jax.experimental import pallas as pl
from jax.experimental.pallas import tpu as pltpu

F32 = jnp.float32
BF16 = jnp.bfloat16

GRID_W = 64
RWKV_HEADS = 4
RWKV_HEAD = 64
RWKV_DIM = RWKV_HEADS * RWKV_HEAD
DECAY_LORA = 64
AAA_LORA = 64
MV_LORA = 32
GATE_LORA = 128
LNX_EPS = 64e-5
ATT_HEADS = 8
ATT_KV_HEADS = 2
ATT_GROUP = ATT_HEADS // ATT_KV_HEADS
ATT_HEAD = 64
ATT_Q_DIM = ATT_HEADS * ATT_HEAD
ATT_KV_DIM = ATT_KV_HEADS * ATT_HEAD
ATT_SCALE = ATT_HEAD ** -0.5
V_BLOCK = 2 * ATT_HEAD
KV_COLS = ATT_KV_DIM + ATT_KV_HEADS * V_BLOCK
WINDOW = 128
ROPE_BASE = 10000.0
ROPE_FREQS = ATT_HEAD // 4
NEG_INF = -1e30
FOURIER_GROUPS = 4
FOURIER_GROUP_DIM = 64
FOURIER_DIM = FOURIER_GROUPS * FOURIER_GROUP_DIM
EPS = 1e-6
RWKV_COLS = 3 * RWKV_DIM + 2 * DECAY_LORA + 2 * AAA_LORA + GATE_LORA

ROW_TILE = 256
ROW_BATCH = 2
HALO = 8
WKV_CHUNK = 64
WKV_BATCH = 4
ATT_BLOCK = 256
FF_CHUNK = 256
MOD_TILE = 1536
VMEM_LIMIT = 56 * 1024 * 1024


def _cparams(n_axes):
    return pltpu.CompilerParams(dimension_semantics=("arbitrary",) * n_axes,
                                vmem_limit_bytes=VMEM_LIMIT)


def _dot(a, b):
    return jnp.dot(a.astype(BF16), b.astype(BF16), preferred_element_type=F32)


def _dot_nt(a, b):
    return lax.dot_general(a.astype(BF16), b.astype(BF16), (((1,), (1,)), ((), ())),
                           preferred_element_type=F32)


def _split_bf16(x):
    hi = x.astype(BF16)
    lo = (x - hi.astype(F32)).astype(BF16)
    return hi, lo


def _dot_exact_rhs(x, m):
    return sum(jnp.dot(part, m, preferred_element_type=F32) for part in _split_bf16(x))


def _dot_exact_lhs(m, x):
    return sum(jnp.dot(m, part, preferred_element_type=F32) for part in _split_bf16(x))


def _sigmoid(x):
    return 1.0 / (1.0 + jnp.exp(-x))


def _const_spec(shape):
    nd = len(shape)
    return pl.BlockSpec(shape, lambda *_: (0,) * nd, pipeline_mode=pl.Buffered(1))


def _tile_with_halo_specs(nb, tm, d, n_own, first):
    hb = tm // HALO
    own = lambda i: jnp.clip(i - first, 0, n_own - 1)
    return [pl.BlockSpec((nb, tm, d), lambda bb, i: (bb, own(i), 0)),
            pl.BlockSpec((nb, HALO, d), lambda bb, i: (bb, jnp.maximum(own(i) * hb - 1, 0), 0)),
            pl.BlockSpec((nb, HALO, d), lambda bb, i: (bb, jnp.minimum((own(i) + 1) * hb, n_own * hb - 1), 0))]


def _rms_mod(x, g, shift, scale):
    y = x * lax.rsqrt(jnp.mean(x * x, axis=-1, keepdims=True) + EPS) * g
    return y * (1.0 + scale) + shift


def _mod_kernel(c_ref, w_ref, b_ref, o_ref):
    x = c_ref[...]
    o_ref[0] = _dot(x * _sigmoid(x), w_ref[0]) + b_ref[0]


def _modulation(cvec, mod_w, mod_b):
    depth, d, n = mod_w.shape
    rows = cvec.shape[0]
    tn = MOD_TILE
    return pl.pallas_call(
        _mod_kernel,
        grid=(depth, n // tn),
        in_specs=[pl.BlockSpec((rows, d), lambda l, j: (0, 0)),
                  pl.BlockSpec((1, d, tn), lambda l, j: (l, 0, j)),
                  pl.BlockSpec((1, 1, tn), lambda l, j: (l, 0, j))],
        out_specs=pl.BlockSpec((1, rows, tn), lambda l, j: (l, 0, j)),
        out_shape=jax.ShapeDtypeStruct((depth, rows, n), F32),
        compiler_params=_cparams(2),
        name="adaln_modulation",
    )(cvec, mod_w.astype(BF16), mod_b.reshape(depth, 1, n))


def _rope(x, cos, sin_signed):
    n = x.shape[1]
    lane = lax.broadcasted_iota(jnp.int32, x.shape, 1)
    first = (lane & ROPE_FREQS) == 0
    partner = jnp.where(first, pltpu.roll(x, n - ROPE_FREQS, 1), pltpu.roll(x, ROPE_FREQS, 1))
    return x * cos + partner * sin_signed


def _inproj_kernel(*refs, split_input, has_vres, n_lat_tiles, n_tiles):
    i = pl.program_id(1)
    if split_input:
        is_lat = i < n_lat_tiles
        lat_refs, ctx_refs = refs[0:3], refs[3:6]
        tile_of = lambda bi, k: jnp.where(is_lat, lat_refs[k][bi], ctx_refs[k][bi])
        refs = refs[3:]
    else:
        lat_refs = refs[0:3]
        tile_of = lambda bi, k: lat_refs[k][bi]
    h_ref = refs[0]
    (mod_ref, g_ref, wr_ref, wq_ref, wkv_ref, wf_ref, wg_ref, dft_ref, cos_ref, sin_ref,
     mu_ref, vec_ref, w0_ref, a0_ref, w2_ref, a2_ref, g2_ref, seg_ref) = refs[3:21]
    refs = refs[21:]
    if has_vres:
        vf_ref, v0_ref, v1_ref, v2_ref = refs[:4]
        refs = refs[4:]
    q_ref, kv_ref, z_ref, ug_ref, fc_ref, ff_ref, fb_ref, ro_ref, buf_ref = refs
    nb, tm, _ = h_ref.shape
    th = tm + 2 * HALO
    prev_ok = jnp.logical_and(i != 0, i != n_lat_tiles)
    next_ok = jnp.logical_and(i != n_lat_tiles - 1, i != n_tiles - 1)
    rid = lax.broadcasted_iota(jnp.int32, (th, 1), 0)
    live = jnp.logical_and(jnp.logical_or(rid >= HALO, prev_ok), jnp.logical_or(rid < HALO + tm, next_ok))
    a_all, a_main = [], []
    for bi in range(nb):
        x = jnp.concatenate([tile_of(bi, 1), tile_of(bi, 0), tile_of(bi, 2)], axis=0)
        a = _rms_mod(x, g_ref[...], mod_ref[bi, 0, 0:1, :], mod_ref[bi, 0, 1:2, :])
        a_all.append(jnp.where(live, a, 0.0).astype(BF16))
        a_main.append(a[HALO:HALO + tm].astype(BF16))
    a_all = jnp.concatenate(a_all, axis=0)
    a = jnp.concatenate(a_main, axis=0)
    cos = jnp.concatenate([cos_ref[...]] * nb, axis=0)
    sin = jnp.concatenate([sin_ref[...]] * nb, axis=0)
    buf_ref[...] = jnp.dot(a_all, wr_ref[...], preferred_element_type=F32).reshape(nb, th, -1)
    q = jnp.dot(a, wq_ref[...], preferred_element_type=F32)
    q_ref[...] = (_rope(q, cos, sin) * ATT_SCALE).astype(BF16).reshape(nb, tm, -1)
    kv = jnp.dot(a, wkv_ref[...], preferred_element_type=F32)
    k = _rope(kv[:, :ATT_KV_DIM], cos[:, :ATT_KV_DIM], sin[:, :ATT_KV_DIM])
    kv_ref[:, :, :ATT_KV_DIM] = k.astype(BF16).reshape(nb, tm, -1)
    v = kv[:, ATT_KV_DIM:]
    low = lax.broadcasted_iota(jnp.int32, v.shape, 1) < ATT_HEAD
    for g in range(ATT_KV_HEADS):
        v_g = v if g == 0 else pltpu.roll(v, ATT_KV_DIM - g * ATT_HEAD, 1)
        kv_ref[:, :, ATT_KV_DIM + g * V_BLOCK:ATT_KV_DIM + (g + 1) * V_BLOCK] = (
            jnp.where(low, v_g, 1.0)[:, :V_BLOCK].astype(BF16).reshape(nb, tm, -1))
    uf = jnp.dot(a, wf_ref[...], preferred_element_type=F32)
    z_ref[...] = _dot(uf, dft_ref[...]).astype(BF16).reshape(nb, tm, -1)
    half = wg_ref.shape[1] // 2
    ug_ref[:, :, :half] = jnp.dot(a, wg_ref[:, :half], preferred_element_type=F32).astype(BF16).reshape(nb, tm, -1)
    _rwkv_features(buf_ref, tm, vf_ref if has_vres else None, (v0_ref, v1_ref, v2_ref) if has_vres else None,
                   mu_ref, vec_ref, w0_ref, a0_ref, w2_ref, a2_ref, g2_ref, seg_ref, fc_ref, ff_ref, fb_ref, ro_ref)
    ug_ref[:, :, half:] = jnp.dot(a, wg_ref[:, half:], preferred_element_type=F32).astype(BF16).reshape(nb, tm, -1)


def _inproj(h, modtab, g, w_in, dft_c, cos_t, sin_t, v_first_src, p, n_lat_tiles):
    split_input = isinstance(h, tuple)
    parts = h if split_input else (h,)
    b, _, d = parts[0].shape
    l = sum(x.shape[1] for x in parts)
    tm = ROW_TILE
    n = RWKV_DIM
    n_tiles = l // tm
    hb = tm // HALO
    n_g = w_in.shape[1] - (RWKV_COLS + ATT_Q_DIM + 2 * ATT_KV_DIM + FOURIER_DIM)
    o = np.cumsum([0, RWKV_COLS, ATT_Q_DIM, 2 * ATT_KV_DIM, FOURIER_DIM, n_g])
    wb = w_in.astype(BF16)
    ws = [wb[:, o[i]:o[i + 1]] for i in range(5)]
    nb = ROW_BATCH
    has_vres = v_first_src is not None
    row = lambda bb, i: (bb, i, 0)
    consts = [g.reshape(1, d)] + ws + [dft_c]
    feat_consts = [p["mu"], p["vec"], p["w0"], p["a0"], p["w2"], p["a2"], p["g2"], p["seg"]]
    in_specs, args, first = [], [], 0
    for x in parts:
        in_specs += _tile_with_halo_specs(nb, tm, d, x.shape[1] // tm, first)
        args += [x, x, x]
        first += x.shape[1] // tm
    in_specs += ([pl.BlockSpec((nb, 1, 6, d), lambda bb, i: (bb, jnp.where(i < n_lat_tiles, 0, 1), 0, 0))]
                 + [_const_spec(x.shape) for x in consts]
                 + [pl.BlockSpec((tm, ATT_Q_DIM), lambda bb, i: (i, 0)),
                    pl.BlockSpec((tm, ATT_Q_DIM), lambda bb, i: (i, 0))]
                 + [_const_spec(x.shape) for x in feat_consts])
    args += [modtab] + consts + [cos_t, sin_t] + feat_consts
    if has_vres:
        vres_consts = [p["v0"], p["v1"], p["v2"]]
        in_specs += [pl.BlockSpec((nb, tm, n), lambda bb, i: (bb, i, 1))] + [_const_spec(x.shape) for x in vres_consts]
        args += [v_first_src] + vres_consts
    outs = [(ATT_Q_DIM, BF16), (KV_COLS, BF16), (2 * FOURIER_DIM, BF16), (n_g, BF16),
            (3 * n, F32), (3 * n, F32), (3 * n, F32), (2 * n, F32)]
    return pl.pallas_call(
        functools.partial(_inproj_kernel, split_input=split_input, has_vres=has_vres, n_lat_tiles=n_lat_tiles,
                          n_tiles=n_tiles),
        grid=(b // nb, n_tiles),
        in_specs=in_specs,
        out_specs=[pl.BlockSpec((nb, tm, w), row) for w, _ in outs],
        out_shape=[jax.ShapeDtypeStruct((b, l, w), dt) for w, dt in outs],
        scratch_shapes=[pltpu.VMEM((nb, tm + 2 * HALO, RWKV_COLS), F32)],
        compiler_params=_cparams(2),
        name="in_projection",
    )(*args)


def _rwkv_features(buf_ref, tm, vf_ref, vres, mu_ref, vec_ref, w0_ref, a0_ref, w2_ref, a2_ref, g2_ref, seg_ref,
                   fc_ref, ff_ref, fb_ref, ro_ref):
    n = RWKV_DIM
    k_k = vec_ref[0:1, :]
    k_a = vec_ref[1:2, :]
    r_k = vec_ref[2:3, :]
    seg = seg_ref[...]
    tiles = []
    for bi in range(buf_ref.shape[0]):
        u = buf_ref[bi, HALO:HALO + tm, :]
        u_prev = buf_ref[bi, HALO - 1:HALO - 1 + tm, :]
        u_next = buf_ref[bi, HALO + 1:HALO + 1 + tm, :]
        us = u + mu_ref[0:1, :] * (u_prev - u) + mu_ref[1:2, :] * (u_next - u)
        o = 3 * n
        t = dict(bi=bi, r=us[:, 0:n], k=us[:, n:2 * n], v=us[:, 2 * n:3 * n])
        t["tw"] = [jnp.tanh(us[:, o + d * DECAY_LORA:o + (d + 1) * DECAY_LORA]).astype(BF16) for d in range(2)]
        o += 2 * DECAY_LORA
        t["xa"] = [us[:, o + d * AAA_LORA:o + (d + 1) * AAA_LORA].astype(BF16) for d in range(2)]
        o += 2 * AAA_LORA
        t["sg"] = _sigmoid(us[:, o:o + GATE_LORA]).astype(BF16)
        t["kk"] = t["k"] * k_k
        t["kk_sq"] = _split_bf16(t["kk"] * t["kk"])
        tiles.append(t)
    for t in tiles:
        t["z"] = [jnp.dot(t["tw"][d], w2_ref[d], preferred_element_type=F32) for d in range(2)]
        t["za"] = [jnp.dot(t["xa"][d], a2_ref[d], preferred_element_type=F32) for d in range(2)]
        t["g"] = jnp.dot(t["sg"], g2_ref[...], preferred_element_type=F32)
        t["ss"] = sum(jnp.dot(part, seg, preferred_element_type=F32) for part in t["kk_sq"])
        if vres is not None:
            t["vv"] = _dot(t["v"], vres[1][...])
    if vres is not None:
        for t in tiles:
            t["vg"] = _dot(t["vv"], vres[2][...])
    for t in tiles:
        bi = t["bi"]
        v = t["v"]
        if vres is not None:
            v = v + (vf_ref[bi] - v) * _sigmoid(vres[0][...] + t["vg"])
        kk = t["kk"] * lax.rsqrt(jnp.maximum(t["ss"], 1e-24))
        keys = []
        for d, out_ref in enumerate((ff_ref, fb_ref)):
            z = w0_ref[d:d + 1, :] + t["z"][d]
            softplus = jnp.maximum(-z, 0.0) + jnp.log(1.0 + jnp.exp(-jnp.abs(z)))
            out_ref[bi, :, 0:n] = -jnp.exp(-softplus - 0.5)
            a = _sigmoid(a0_ref[d:d + 1, :] + t["za"][d])
            key = t["k"] * (1.0 + (a - 1.0) * k_a)
            keys.append(key)
            out_ref[bi, :, n:2 * n] = key
            out_ref[bi, :, 2 * n:3 * n] = kk * a
        t["v"] = v
        t["bonus_in"] = _split_bf16(t["r"] * (0.5 * (keys[0] + keys[1])) * r_k)
        fc_ref[bi, :, 0:n] = t["r"]
        fc_ref[bi, :, n:2 * n] = v
        fc_ref[bi, :, 2 * n:3 * n] = kk
        ro_ref[bi, :, n:2 * n] = t["g"]
    for t in tiles:
        bonus = sum(jnp.dot(part, seg, preferred_element_type=F32) for part in t["bonus_in"])
        ro_ref[t["bi"], :, 0:n] = bonus * t["v"]


def _wkv_masks(c, reverse):
    ti = lax.broadcasted_iota(jnp.int32, (c, c), 0)
    si = lax.broadcasted_iota(jnp.int32, (c, c), 1)
    incl, strict = (si >= ti, si > ti) if reverse else (si <= ti, si < ti)
    levels = []
    for sh in range(int(math.log2(c))):
        bt = lax.shift_right_logical(ti, sh)
        bs = lax.shift_right_logical(si, sh)
        if reverse:
            levels.append(jnp.logical_and((bt & 1) == 0, bs == bt + 1))
        else:
            levels.append(jnp.logical_and((bt & 1) == 1, bs == bt - 1))
    return incl, strict, levels, jnp.where(ti == si, 1.0, 0.0)


def _wkv_kernel(fcf_ref, ff_ref, fcb_ref, fb_ref, yf_ref, yb_ref, sf_ref, sb_ref):
    @pl.when(pl.program_id(1) == 0)
    def _():
        sf_ref[...] = jnp.zeros_like(sf_ref)
        sb_ref[...] = jnp.zeros_like(sb_ref)

    c = WKV_CHUNK
    n = RWKV_DIM
    hd = RWKV_HEAD
    nb = fcf_ref.shape[0]
    probs = []
    for fc_ref, fd_ref, y_ref, s_ref, reverse in ((fcf_ref, ff_ref, yf_ref, sf_ref, False),
                                                  (fcb_ref, fb_ref, yb_ref, sb_ref, True)):
        incl, strict, levels, eye = _wkv_masks(c, reverse)
        tri = jnp.where(incl, 1.0, 0.0).astype(BF16)
        cums = [_dot_exact_lhs(tri, fd_ref[bi, :, 0:n]) for bi in range(nb)]
        for bi in range(nb):
            cum = cums[bi]
            total = cum[0:1, :] if reverse else cum[c - 1:c, :]
            c0 = 0.5 * total
            e_neg = jnp.exp(c0 - cum)
            a_t = -fc_ref[bi, :, 2 * n:3 * n] * jnp.exp(cum - fd_ref[bi, :, 0:n] - c0)
            r_t = fc_ref[bi, :, 0:n] * jnp.exp(cum - c0)
            b_t = fd_ref[bi, :, 2 * n:3 * n] * e_neg
            k_t = fd_ref[bi, :, n:2 * n] * e_neg
            e_half = jnp.exp(c0)
            e_tot = jnp.exp(total)
            for h in range(RWKV_HEADS):
                sl = slice(h * hd, (h + 1) * hd)
                probs.append(dict(
                    lhs=jnp.concatenate([a_t[:, sl], r_t[:, sl]], axis=0).astype(BF16),
                    rhs=jnp.concatenate([b_t[:, sl], k_t[:, sl]], axis=0).astype(BF16),
                    v=fc_ref[bi, :, n + h * hd:n + (h + 1) * hd], s0=s_ref[bi, h],
                    e_half=e_half[:, sl], e_tot=e_tot[:, sl], incl=incl, strict=strict, levels=levels, eye=eye,
                    y_ref=y_ref, s_ref=s_ref, bi=bi, h=h, sl=sl))

    for p in probs:
        p["g"] = _dot_nt(p["lhs"], p["rhs"])
    for p in probs:
        p["a_s"] = _dot_nt(p["lhs"], p["s0"] * p["e_half"])
    for p in probs:
        g = p["g"]
        p["a_ab"] = jnp.where(p["strict"], g[:c, :c], 0.0)
        a_ak = jnp.where(p["strict"], g[:c, c:], 0.0)
        p["a_r"] = jnp.concatenate([jnp.where(p["incl"], g[c:, :c], 0.0),
                                    jnp.where(p["incl"], g[c:, c:], 0.0)], axis=1).astype(BF16)
        p["t"] = p["eye"] + jnp.where(p["levels"][0], p["a_ab"], 0.0)
        p["rhs_u"] = p["a_s"][:c] + _dot(a_ak, p["v"])
    for lvl in range(1, len(probs[0]["levels"])):
        for p in probs:
            p["tb"] = p["t"].astype(BF16)
            p["tmp"] = _dot(jnp.where(p["levels"][lvl], p["a_ab"], 0.0), p["tb"])
        for p in probs:
            p["t"] = p["t"] + _dot(p["tb"], p["tmp"])
    for p in probs:
        p["uv"] = jnp.concatenate([_dot(p["t"], p["rhs_u"]), p["v"]], axis=0).astype(BF16)
    for p in probs:
        p["y_ref"][p["bi"], :, p["sl"]] = p["a_s"][c:] + _dot(p["a_r"], p["uv"])
    for p in probs:
        upd = lax.dot_general(p["uv"], p["rhs"], (((0,), (0,)), ((), ())), preferred_element_type=F32)
        p["s_ref"][p["bi"], p["h"]] = p["s0"] * p["e_tot"] + upd * p["e_half"]


def _wkv_scan(fc, ff, fb, n_lat):
    b, l, w = fc.shape
    c = WKV_CHUNK
    nb = WKV_BATCH
    nl = n_lat // c
    nc = l // c - nl
    fwd = lambda bb, j: (bb, jnp.where(j < nc, nl + j, j - nc), 0)
    bwd = lambda bb, j: (bb, nl + nc - 1 - j, 0)
    return pl.pallas_call(
        _wkv_kernel,
        grid=(b // nb, nl + nc),
        in_specs=[pl.BlockSpec((nb, c, w), fwd), pl.BlockSpec((nb, c, w), fwd),
                  pl.BlockSpec((nb, c, w), bwd), pl.BlockSpec((nb, c, w), bwd)],
        out_specs=[pl.BlockSpec((nb, c, RWKV_DIM), fwd), pl.BlockSpec((nb, c, RWKV_DIM), bwd)],
        out_shape=[jax.ShapeDtypeStruct((b, l, RWKV_DIM), F32)] * 2,
        scratch_shapes=[pltpu.VMEM((nb, RWKV_HEADS, RWKV_HEAD, RWKV_HEAD), F32)] * 2,
        compiler_params=_cparams(2),
        name="wkv_scan",
    )(fc, ff, fc, fb)


def _attn_kernel(sink_ref, q_ref, kv_ref, o_ref, *, n_lat):
    j = pl.program_id(1)
    qb = ATT_BLOCK
    hd = ATT_HEAD
    n_ctx = kv_ref.shape[1] - n_lat
    n_win = qb + 2 * WINDOW
    is_lat = j * qb < n_lat
    ws = pl.multiple_of(jnp.clip(j * qb - WINDOW, 0, n_lat - n_win), WINDOW)
    col = lax.broadcasted_iota(jnp.int32, (qb, n_ctx + n_win), 1)
    q_pos = j * qb + lax.broadcasted_iota(jnp.int32, (qb, n_ctx + n_win), 0)
    k_pos = ws + col - n_ctx
    valid = jnp.logical_or(col < n_ctx, jnp.logical_and(jnp.abs(q_pos - k_pos) <= WINDOW, is_lat))
    q = q_ref[0]
    kv = jnp.concatenate([kv_ref[0, n_lat:n_lat + n_ctx, :], kv_ref[0, pl.ds(ws, n_win), :]], axis=0)
    heads = range(ATT_HEADS)
    s = [jnp.where(valid, _dot_nt(q[:, h * hd:(h + 1) * hd],
                                  kv[:, (h // ATT_GROUP) * hd:(h // ATT_GROUP + 1) * hd]), NEG_INF) for h in heads]
    m = [jnp.maximum(jnp.max(s[h], axis=-1, keepdims=True), sink_ref[h]) for h in heads]
    p = [jnp.exp((s[h] - m[h]).astype(BF16)) for h in heads]
    o = [jnp.dot(p[h], kv[:, ATT_KV_DIM + (h // ATT_GROUP) * V_BLOCK:ATT_KV_DIM + (h // ATT_GROUP + 1) * V_BLOCK],
                 preferred_element_type=F32) for h in heads]
    for h in heads:
        den = pltpu.roll(o[h], hd, 1) + jnp.exp(sink_ref[h] - m[h])
        o_ref[0, :, h * hd:(h + 1) * hd] = (o[h] / den)[:, :hd].astype(o_ref.dtype)


def _attention(q, kv, sink, n_lat, n_rows):
    b, l, _ = q.shape
    qb = ATT_BLOCK
    return pl.pallas_call(
        functools.partial(_attn_kernel, n_lat=n_lat),
        grid=(b, n_rows // qb),
        in_specs=[pl.BlockSpec(memory_space=pltpu.SMEM),
                  pl.BlockSpec((1, qb, ATT_Q_DIM), lambda bb, j: (bb, j, 0)),
                  pl.BlockSpec((1, l, kv.shape[2]), lambda bb, j: (bb, 0, 0))],
        out_specs=pl.BlockSpec((1, qb, ATT_Q_DIM), lambda bb, j: (bb, j, 0)),
        out_shape=jax.ShapeDtypeStruct((b, n_rows, ATT_Q_DIM), BF16),
        compiler_params=_cparams(2),
        name="windowed_attention",
    )(sink, q, kv)


def _dft_kernel(z_ref, ct_ref, st_ref, o_ref):
    n = FOURIER_DIM
    o_ref[0] = (jnp.dot(ct_ref[...], z_ref[0, :, 0:n], preferred_element_type=F32)
                - jnp.dot(st_ref[...], z_ref[0, :, n:2 * n], preferred_element_type=F32)).astype(o_ref.dtype)


def _token_dft(z, ct, st, seg_rows, seg_block):
    b = z.shape[0]
    return pl.pallas_call(
        _dft_kernel,
        grid=(b,),
        in_specs=[pl.BlockSpec((1, seg_rows, z.shape[2]), lambda bb: (bb, seg_block, 0)),
                  _const_spec(ct.shape), _const_spec(st.shape)],
        out_specs=pl.BlockSpec((1, seg_rows, FOURIER_DIM), lambda bb: (bb, 0, 0)),
        out_shape=jax.ShapeDtypeStruct((b, seg_rows, FOURIER_DIM), BF16),
        compiler_params=_cparams(1),
        name="token_dft",
    )(z, ct, st)


def _merge_kernel(*refs, split_input, n_lat_tiles):
    if split_input:
        is_lat = pl.program_id(1) < n_lat_tiles
        x_ref, c_ref = refs[0:2]
        residual = lambda bi: jnp.where(is_lat, x_ref[bi], c_ref[bi])
        refs = refs[1:]
    else:
        residual = lambda bi: h_ref[bi]
    (h_ref, mod_ref, yf_ref, yb_ref, ro_ref, ya_ref, yd_ref, ug_ref, ln_ref, avg_ref,
     wbr_ref, wba_ref, wbf_ref, wo_ref, gp_ref, o_ref) = refs
    n = RWKV_DIM
    d = h_ref.shape[2]
    avg = avg_ref[...]
    nb, tm, _ = h_ref.shape
    rows = nb * tm
    y = (yf_ref[...] + yb_ref[...]).reshape(rows, n)
    mean = _dot_exact_rhs(y, avg)
    mix = (_sigmoid(ug_ref[:, :, d:2 * d].reshape(rows, d).astype(F32))
           * jnp.dot(ya_ref[...].reshape(rows, -1), wba_ref[...], preferred_element_type=F32))
    dev = y - mean
    var = _dot_exact_rhs(dev * dev, avg)
    mix += (_sigmoid(ug_ref[:, :, 2 * d:3 * d].reshape(rows, d).astype(F32))
            * jnp.dot(yd_ref[...].reshape(rows, -1), wbf_ref[...], preferred_element_type=F32))
    yn = dev * lax.rsqrt(var + LNX_EPS) * ln_ref[0:1, :] + ln_ref[1:2, :]
    ro = ro_ref[...].reshape(rows, 2 * n)
    y_r = (yn + ro[:, 0:n]) * ro[:, n:2 * n]
    mix += _sigmoid(ug_ref[:, :, 0:d].reshape(rows, d).astype(F32)) * _dot(y_r, wbr_ref[...])
    o = _dot(mix, wo_ref[...])
    o = o * lax.rsqrt(jnp.mean(o * o, axis=-1, keepdims=True) + EPS) * gp_ref[...]
    for bi in range(nb):
        o_ref[bi] = residual(bi) + mod_ref[bi, 0, 2:3, :] * o[bi * tm:(bi + 1) * tm]


def _merge(h, modtab, yf, yb, ro, ya, yd, ug, p, n_rows, n_lat_tiles):
    split_input = isinstance(h, tuple)
    parts = h if split_input else (h,)
    b, _, d = parts[0].shape
    tm = ROW_TILE
    nb = ROW_BATCH
    row = lambda bb, i: (bb, i, 0)
    consts = [p["ln"], p["avg"], p["wbr"], p["wba"], p["wbf"], p["wo"], p["gpost"]]
    h_specs, first = [], 0
    for x in parts:
        h_specs.append(_tile_with_halo_specs(nb, tm, d, x.shape[1] // tm, first)[0])
        first += x.shape[1] // tm
    return pl.pallas_call(
        functools.partial(_merge_kernel, split_input=split_input, n_lat_tiles=n_lat_tiles),
        grid=(b // nb, n_rows // tm),
        in_specs=h_specs + [
                  pl.BlockSpec((nb, 1, 6, d), lambda bb, i: (bb, jnp.where(i < n_lat_tiles, 0, 1), 0, 0)),
                  pl.BlockSpec((nb, tm, RWKV_DIM), row), pl.BlockSpec((nb, tm, RWKV_DIM), row),
                  pl.BlockSpec((nb, tm, 2 * RWKV_DIM), row),
                  pl.BlockSpec((nb, tm, ATT_Q_DIM), row), pl.BlockSpec((nb, tm, FOURIER_DIM), row),
                  pl.BlockSpec((nb, tm, 3 * d), row)] + [_const_spec(x.shape) for x in consts],
        out_specs=pl.BlockSpec((nb, tm, d), row),
        out_shape=jax.ShapeDtypeStruct((b, n_rows, d), F32),
        compiler_params=_cparams(2),
        name="branch_merge",
    )(*parts, modtab, yf, yb, ro, ya, yd, ug, *consts)


def _ffn_kernel(h_ref, hp_ref, hn_ref, mod_ref, gpre_ref, upg_ref, upv_ref, cw_ref, dn_ref, gpost_ref, o_ref,
                zg_ref, act_ref, *, n_lat_tiles, n_tiles):
    i = pl.program_id(1)
    tm = h_ref.shape[1]
    prev_ok = jnp.logical_and(i != 0, i != n_lat_tiles)
    next_ok = jnp.logical_and(i != n_lat_tiles - 1, i != n_tiles - 1)
    rid = lax.broadcasted_iota(jnp.int32, (tm + 2 * HALO, 1), 0)
    live = jnp.logical_and(jnp.logical_or(rid >= HALO, prev_ok), jnp.logical_or(rid < HALO + tm, next_ok))
    nb = h_ref.shape[0]
    th = tm + 2 * HALO
    f_all, f_main = [], []
    for bi in range(nb):
        x = jnp.concatenate([hp_ref[bi], h_ref[bi], hn_ref[bi]], axis=0)
        f = _rms_mod(x, gpre_ref[...], mod_ref[bi, 0, 3:4, :], mod_ref[bi, 0, 4:5, :])
        f_all.append(jnp.where(live, f, 0.0).astype(BF16))
        f_main.append(f[HALO:HALO + tm].astype(BF16))
    f_all = jnp.concatenate(f_all, axis=0)
    f_main = jnp.concatenate(f_main, axis=0)
    for c in range(zg_ref.shape[2] // FF_CHUNK):
        cols = slice(c * FF_CHUNK, (c + 1) * FF_CHUNK)
        zg_ref[:, :, cols] = jnp.dot(f_all, upg_ref[:, cols], preferred_element_type=F32).reshape(nb, th, -1)
        zv = jnp.dot(f_main, upv_ref[:, cols], preferred_element_type=F32).reshape(nb, tm, -1)
        zg = (cw_ref[0:1, cols] * zg_ref[:, HALO - 1:HALO - 1 + tm, cols]
              + cw_ref[1:2, cols] * zg_ref[:, HALO:HALO + tm, cols]
              + cw_ref[2:3, cols] * zg_ref[:, HALO + 1:HALO + 1 + tm, cols] + cw_ref[3:4, cols])
        act = 0.5 * zg * (1.0 + jnp.tanh(0.7978845608028654 * (zg + 0.044715 * zg * zg * zg)))
        act_ref[:, :, cols] = (act * zv).astype(BF16)
    o = jnp.dot(act_ref[...].reshape(nb * tm, -1), dn_ref[...], preferred_element_type=F32)
    o = o * lax.rsqrt(jnp.mean(o * o, axis=-1, keepdims=True) + EPS) * gpost_ref[...]
    for bi in range(nb):
        o_ref[bi] = h_ref[bi] + mod_ref[bi, 0, 5:6, :] * o[bi * tm:(bi + 1) * tm]


def _ffn(h, modtab, p, n_rows, n_lat_tiles):
    b, l, d = h.shape
    tm = ROW_TILE
    hb = tm // HALO
    n_tiles = l // tm
    d_ff = p["dn"].shape[0]
    nb = ROW_BATCH
    row = lambda bb, i: (bb, i, 0)
    consts = [p["gpre"], p["upg"], p["upv"], p["cw"], p["dn"], p["gpost"]]
    return pl.pallas_call(
        functools.partial(_ffn_kernel, n_lat_tiles=n_lat_tiles, n_tiles=n_tiles),
        grid=(b // nb, n_rows // tm),
        in_specs=[pl.BlockSpec((nb, tm, d), row),
                  pl.BlockSpec((nb, HALO, d), lambda bb, i: (bb, jnp.maximum(i * hb - 1, 0), 0)),
                  pl.BlockSpec((nb, HALO, d), lambda bb, i: (bb, jnp.minimum((i + 1) * hb, l // HALO - 1), 0)),
                  pl.BlockSpec((nb, 1, 6, d), lambda bb, i: (bb, jnp.where(i < n_lat_tiles, 0, 1), 0, 0))]
        + [_const_spec(x.shape) for x in consts],
        out_specs=pl.BlockSpec((nb, tm, d), row),
        out_shape=jax.ShapeDtypeStruct((b, n_rows, d), F32),
        scratch_shapes=[pltpu.VMEM((nb, tm + 2 * HALO, d_ff), F32), pltpu.VMEM((nb, tm, d_ff), BF16)],
        compiler_params=_cparams(2),
        name="conv_ffn",
    )(h, h, h, modtab, *consts)


def _rope_tables(n_lat, n_ctx):
    t = jnp.arange(n_lat)
    row_id = (t // GRID_W).astype(F32)
    col_id = (t % GRID_W).astype(F32)
    inv = ROPE_BASE ** (-jnp.arange(ROPE_FREQS, dtype=F32) / ROPE_FREQS)
    d = np.arange(ATT_HEAD)
    freq = d % ROPE_FREQS
    ang = jnp.where((d // (2 * ROPE_FREQS) == 0)[None, :], row_id[:, None], col_id[:, None]) * inv[freq][None, :]
    sign = np.where((d // ROPE_FREQS) % 2 == 0, -1.0, 1.0).astype(np.float32)
    cos = jnp.concatenate([jnp.cos(ang), jnp.ones((n_ctx, ATT_HEAD), F32)], axis=0)
    sin = jnp.concatenate([jnp.sin(ang) * sign[None, :], jnp.zeros((n_ctx, ATT_HEAD), F32)], axis=0)
    return jnp.tile(cos, (1, ATT_HEADS)), jnp.tile(sin, (1, ATT_HEADS))


def _dft_mats(n):
    r = 1 << (int(math.log2(n)) // 2)
    u = jnp.arange(n, dtype=jnp.int32)[None, :]

    def table(t):
        ang = ((t[:, None] * u) % n).astype(F32) * (2.0 * math.pi / n)
        return jnp.cos(ang), jnp.sin(ang)

    c_hi, s_hi = table(jnp.arange(n // r, dtype=jnp.int32) * r)
    c_lo, s_lo = table(jnp.arange(r, dtype=jnp.int32))
    scale = 1.0 / math.sqrt(n)
    c_hi, s_hi = c_hi[:, None, :] * scale, s_hi[:, None, :] * scale
    cos = (c_hi * c_lo[None] - s_hi * s_lo[None]).reshape(n, n)
    sin = (s_hi * c_lo[None] + c_hi * s_lo[None]).reshape(n, n)
    return cos, sin


def _block_diag(m, groups):
    return jnp.kron(jnp.eye(groups, dtype=m.dtype), m)


def kernel(x, c, ctx, c_ctx, mod_w, mod_b, norm_mix_pre, norm_mix_post, norm_ffn_pre, norm_ffn_post, w_in, rwkv_mu, rwkv_w0, rwkv_w2, rwkv_a0, rwkv_a2, rwkv_g2, rwkv_k_k, rwkv_k_a, rwkv_r_k, rwkv_lnx_w, rwkv_lnx_b, rwkv_v0, rwkv_v1, rwkv_v2, attn_sink, w_branch_rwkv, w_branch_attn, w_branch_fourier, w_out, ffn_up, ffn_conv_w, ffn_conv_b, ffn_down):
    b, n_lat, d = x.shape
    n_ctx = ctx.shape[1]
    depth = mod_w.shape[0]
    d_ff = ffn_down.shape[1]
    l = n_lat + n_ctx
    tm = ROW_TILE
    assert n_lat % tm == 0 and n_ctx % tm == 0 and n_lat % n_ctx == 0
    assert n_lat >= ATT_BLOCK + 2 * WINDOW and n_lat % ATT_BLOCK == 0 and n_ctx % ATT_BLOCK == 0
    assert ATT_BLOCK % WINDOW == 0 and d_ff % FF_CHUNK == 0 and b % WKV_BATCH == 0 and b % ROW_BATCH == 0
    n_lat_tiles = n_lat // tm

    cos_t, sin_t = _rope_tables(n_lat, n_ctx)
    cg, sg = _dft_mats(FOURIER_GROUP_DIM)
    dft_c = jnp.concatenate([_block_diag(cg, FOURIER_GROUPS), _block_diag(sg, FOURIER_GROUPS)], axis=1).astype(BF16)
    ct_lat, st_lat = (m.astype(BF16) for m in _dft_mats(n_lat))
    ct_ctx, st_ctx = (m.astype(BF16) for m in _dft_mats(n_ctx))
    seg = _block_diag(jnp.ones((RWKV_HEAD, RWKV_HEAD), F32), RWKV_HEADS).astype(BF16)
    avg = (seg.astype(F32) / RWKV_HEAD).astype(BF16)

    pad = (-(b + 1)) % 8
    cvec = jnp.concatenate([c, c_ctx[None, :], jnp.zeros((pad, d), F32)], axis=0)
    mod = _modulation(cvec, mod_w, mod_b)

    h = (x, ctx)
    v_first = None
    for layer in range(depth):
        last = layer == depth - 1
        lat = mod[layer, :b].reshape(b, 1, 6, d)
        cm = jnp.broadcast_to(mod[layer, b].reshape(1, 1, 6, d), (b, 1, 6, d))
        modtab = jnp.concatenate([lat, cm], axis=1)

        fp = {
            "mu": rwkv_mu[layer],
            "vec": jnp.stack([rwkv_k_k[layer], rwkv_k_a[layer], rwkv_r_k[layer]]),
            "w0": rwkv_w0[layer], "a0": rwkv_a0[layer],
            "w2": rwkv_w2[layer].astype(BF16), "a2": rwkv_a2[layer].astype(BF16),
            "g2": rwkv_g2[layer].astype(BF16), "seg": seg,
        }
        if layer > 0:
            lp = 128 - MV_LORA
            fp["v0"] = rwkv_v0[layer - 1].reshape(1, RWKV_DIM)
            fp["v1"] = jnp.pad(rwkv_v1[layer - 1], ((0, 0), (0, lp))).astype(BF16)
            fp["v2"] = jnp.pad(rwkv_v2[layer - 1], ((0, lp), (0, 0))).astype(BF16)
        q, kv, z, u_g, fc, ff, fb, ro = _inproj(h, modtab, norm_mix_pre[layer], w_in[layer], dft_c, cos_t, sin_t,
                                                v_first, fp, n_lat_tiles)
        if layer == 0:
            v_first = fc
        y_fwd, y_bwd = _wkv_scan(fc, ff, fb, n_lat)

        n_rows = n_lat if last else l
        y_att = _attention(q, kv, attn_sink[layer], n_lat, n_rows)
        y_dft = _token_dft(z, ct_lat, st_lat, n_lat, 0)
        if not last:
            y_dft = jnp.concatenate([y_dft, _token_dft(z, ct_ctx, st_ctx, n_ctx, n_lat // n_ctx)], axis=1)

        mp = {
            "ln": jnp.stack([rwkv_lnx_w[layer], rwkv_lnx_b[layer]]), "avg": avg,
            "wbr": w_branch_rwkv[layer].astype(BF16), "wba": w_branch_attn[layer].astype(BF16),
            "wbf": w_branch_fourier[layer].astype(BF16), "wo": w_out[layer].astype(BF16),
            "gpost": norm_mix_post[layer].reshape(1, d),
        }
        h = _merge(h, modtab, y_fwd, y_bwd, ro, y_att, y_dft, u_g, mp, n_rows, n_lat_tiles)

        up = ffn_up[layer].astype(BF16)
        pp = {
            "gpre": norm_ffn_pre[layer].reshape(1, d),
            "upg": up[:, :d_ff], "upv": up[:, d_ff:],
            "cw": jnp.concatenate([ffn_conv_w[layer], ffn_conv_b[layer][None, :]], axis=0),
            "dn": ffn_down[layer].astype(BF16),
            "gpost": norm_ffn_post[layer].reshape(1, d),
        }
        h = _ffn(h, modtab, pp, n_rows, n_lat_tiles)
    return h
```

```python
import functools
import math

import numpy as np
import jax
import jax.numpy as jnp
from jax import lax
from jax.experimental import pallas as pl
from jax.experimental.pallas import tpu as pltpu

F32 = jnp.float32
BF16 = jnp.bfloat16

GRID_W = 64
RWKV_HEADS = 4
RWKV_HEAD = 64
RWKV_DIM = RWKV_HEADS * RWKV_HEAD
DECAY_LORA = 64
AAA_LORA = 64
MV_LORA = 32
GATE_LORA = 128
LNX_EPS = 64e-5
ATT_HEADS = 8
ATT_KV_HEADS = 2
ATT_GROUP = ATT_HEADS // ATT_KV_HEADS
ATT_HEAD = 64
ATT_Q_DIM = ATT_HEADS * ATT_HEAD
ATT_KV_DIM = ATT_KV_HEADS * ATT_HEAD
ATT_SCALE = ATT_HEAD ** -0.5
V_BLOCK = 2 * ATT_HEAD
KV_COLS = ATT_KV_DIM + ATT_KV_HEADS * V_BLOCK
WINDOW = 128
ROPE_BASE = 10000.0
ROPE_FREQS = ATT_HEAD // 4
NEG_INF = -1e30
FOURIER_GROUPS = 4
FOURIER_GROUP_DIM = 64
FOURIER_DIM = FOURIER_GROUPS * FOURIER_GROUP_DIM
EPS = 1e-6
RWKV_COLS = 3 * RWKV_DIM + 2 * DECAY_LORA + 2 * AAA_LORA + GATE_LORA

ROW_TILE = 256
ROW_BATCH = 2
HALO = 8
WKV_CHUNK = 64
WKV_BATCH = 4
ATT_BLOCK = 256
FF_CHUNK = 256
VMEM_LIMIT = 56 * 1024 * 1024


def _cparams(n_axes):
    return pltpu.CompilerParams(dimension_semantics=("arbitrary",) * n_axes,
                                vmem_limit_bytes=VMEM_LIMIT)


def _dot(a, b):
    return jnp.dot(a.astype(BF16), b.astype(BF16), preferred_element_type=F32)


def _dot_nt(a, b):
    return lax.dot_general(a.astype(BF16), b.astype(BF16), (((1,), (1,)), ((), ())),
                           preferred_element_type=F32)


def _split3(x):
    hi = x.astype(BF16)
    r1 = x - hi.astype(F32)
    mid = r1.astype(BF16)
    lo = (r1 - mid.astype(F32)).astype(BF16)
    return hi, mid, lo


def _dot_exact_rhs(x, m):
    hi, mid, lo = _split3(x)
    return (jnp.dot(hi, m, preferred_element_type=F32) + jnp.dot(mid, m, preferred_element_type=F32)
            + jnp.dot(lo, m, preferred_element_type=F32))


def _dot_exact_lhs(m, x):
    hi, mid, lo = _split3(x)
    return (jnp.dot(m, hi, preferred_element_type=F32) + jnp.dot(m, mid, preferred_element_type=F32)
            + jnp.dot(m, lo, preferred_element_type=F32))


def _sigmoid(x):
    return 1.0 / (1.0 + jnp.exp(-x))


def _const_spec(shape):
    nd = len(shape)
    return pl.BlockSpec(shape, lambda *_: (0,) * nd, pipeline_mode=pl.Buffered(1))


def _tile_with_halo_specs(nb, tm, d, n_own, first):
    hb = tm // HALO
    own = lambda i: jnp.clip(i - first, 0, n_own - 1)
    return [pl.BlockSpec((nb, tm, d), lambda bb, i: (bb, own(i), 0)),
            pl.BlockSpec((nb, HALO, d), lambda bb, i: (bb, jnp.maximum(own(i) * hb - 1, 0), 0)),
            pl.BlockSpec((nb, HALO, d), lambda bb, i: (bb, jnp.minimum((own(i) + 1) * hb, n_own * hb - 1), 0))]


def _rms_mod(x, g, shift, scale):
    y = x * lax.rsqrt(jnp.mean(x * x, axis=-1, keepdims=True) + EPS) * g
    return y * (1.0 + scale) + shift


def _mod_kernel(c_ref, w_ref, b_ref, o_ref):
    x = c_ref[...]
    o_ref[0] = _dot(x * _sigmoid(x), w_ref[0]) + b_ref[0]


def _modulation(cvec, mod_w, mod_b):
    depth, d, n = mod_w.shape
    rows = cvec.shape[0]
    tn = 1536
    return pl.pallas_call(
        _mod_kernel,
        grid=(depth, n // tn),
        in_specs=[pl.BlockSpec((rows, d), lambda l, j: (0, 0)),
                  pl.BlockSpec((1, d, tn), lambda l, j: (l, 0, j)),
                  pl.BlockSpec((1, 1, tn), lambda l, j: (l, 0, j))],
        out_specs=pl.BlockSpec((1, rows, tn), lambda l, j: (l, 0, j)),
        out_shape=jax.ShapeDtypeStruct((depth, rows, n), F32),
        compiler_params=_cparams(2),
        name="adaln_modulation",
    )(cvec, mod_w.astype(BF16), mod_b.reshape(depth, 1, n))


def _rope(x, cos, sin_signed):
    n = x.shape[1]
    lane = lax.broadcasted_iota(jnp.int32, x.shape, 1)
    first = (lane & ROPE_FREQS) == 0
    partner = jnp.where(first, pltpu.roll(x, n - ROPE_FREQS, 1), pltpu.roll(x, ROPE_FREQS, 1))
    return x * cos + partner * sin_signed


def _inproj_kernel(*refs, split_input, has_vres, n_lat_tiles, n_tiles):
    i = pl.program_id(1)
    if split_input:
        is_lat = i < n_lat_tiles
        lat_refs, ctx_refs = refs[0:3], refs[3:6]
        tile_of = lambda bi, k: jnp.where(is_lat, lat_refs[k][bi], ctx_refs[k][bi])
        refs = refs[3:]
    else:
        lat_refs = refs[0:3]
        tile_of = lambda bi, k: lat_refs[k][bi]
    h_ref = refs[0]
    (mod_ref, g_ref, wr_ref, wq_ref, wkv_ref, wf_ref, wg_ref, dft_ref, cos_ref, sin_ref,
     mu_ref, vec_ref, w0_ref, a0_ref, w2_ref, a2_ref, g2_ref, seg_ref) = refs[3:21]
    refs = refs[21:]
    if has_vres:
        vf_ref, v0_ref, v1_ref, v2_ref = refs[:4]
        refs = refs[4:]
    q_ref, kv_ref, z_ref, ug_ref, fc_ref, ff_ref, fb_ref, ro_ref, buf_ref = refs
    nb, tm, _ = h_ref.shape
    th = tm + 2 * HALO
    prev_ok = jnp.logical_and(i != 0, i != n_lat_tiles)
    next_ok = jnp.logical_and(i != n_lat_tiles - 1, i != n_tiles - 1)
    rid = lax.broadcasted_iota(jnp.int32, (th, 1), 0)
    live = jnp.logical_and(jnp.logical_or(rid >= HALO, prev_ok), jnp.logical_or(rid < HALO + tm, next_ok))
    a_all, a_main = [], []
    for bi in range(nb):
        x = jnp.concatenate([tile_of(bi, 1), tile_of(bi, 0), tile_of(bi, 2)], axis=0)
        a = _rms_mod(x, g_ref[...], mod_ref[bi, 0, 0:1, :], mod_ref[bi, 0, 1:2, :])
        a_all.append(jnp.where(live, a, 0.0).astype(BF16))
        a_main.append(a[HALO:HALO + tm].astype(BF16))
    a_all = jnp.concatenate(a_all, axis=0)
    a = jnp.concatenate(a_main, axis=0)
    cos = jnp.concatenate([cos_ref[...]] * nb, axis=0)
    sin = jnp.concatenate([sin_ref[...]] * nb, axis=0)
    buf_ref[...] = jnp.dot(a_all, wr_ref[...], preferred_element_type=F32).reshape(nb, th, -1)
    q = jnp.dot(a, wq_ref[...], preferred_element_type=F32)
    q_ref[...] = (_rope(q, cos, sin) * ATT_SCALE).astype(BF16).reshape(nb, tm, -1)
    kv = jnp.dot(a, wkv_ref[...], preferred_element_type=F32)
    k = _rope(kv[:, :ATT_KV_DIM], cos[:, :ATT_KV_DIM], sin[:, :ATT_KV_DIM])
    kv_ref[:, :, :ATT_KV_DIM] = k.astype(BF16).reshape(nb, tm, -1)
    v = kv[:, ATT_KV_DIM:]
    low = lax.broadcasted_iota(jnp.int32, v.shape, 1) < ATT_HEAD
    for g in range(ATT_KV_HEADS):
        v_g = v if g == 0 else pltpu.roll(v, ATT_KV_DIM - g * ATT_HEAD, 1)
        kv_ref[:, :, ATT_KV_DIM + g * V_BLOCK:ATT_KV_DIM + (g + 1) * V_BLOCK] = (
            jnp.where(low, v_g, 1.0)[:, :V_BLOCK].astype(BF16).reshape(nb, tm, -1))
    uf = jnp.dot(a, wf_ref[...], preferred_element_type=F32)
    z_ref[...] = _dot(uf, dft_ref[...]).astype(BF16).reshape(nb, tm, -1)
    half = wg_ref.shape[1] // 2
    ug_ref[:, :, :half] = jnp.dot(a, wg_ref[:, :half], preferred_element_type=F32).astype(BF16).reshape(nb, tm, -1)
    _rwkv_features(buf_ref, tm, vf_ref if has_vres else None, (v0_ref, v1_ref, v2_ref) if has_vres else None,
                   mu_ref, vec_ref, w0_ref, a0_ref, w2_ref, a2_ref, g2_ref, seg_ref, fc_ref, ff_ref, fb_ref, ro_ref)
    ug_ref[:, :, half:] = jnp.dot(a, wg_ref[:, half:], preferred_element_type=F32).astype(BF16).reshape(nb, tm, -1)


def _inproj(h, modtab, g, w_in, dft_c, cos_t, sin_t, v_first_src, p, n_lat_tiles):
    split_input = isinstance(h, tuple)
    parts = h if split_input else (h,)
    b, _, d = parts[0].shape
    l = sum(x.shape[1] for x in parts)
    tm = ROW_TILE
    n = RWKV_DIM
    n_tiles = l // tm
    hb = tm // HALO
    n_g = w_in.shape[1] - (RWKV_COLS + ATT_Q_DIM + 2 * ATT_KV_DIM + FOURIER_DIM)
    o = np.cumsum([0, RWKV_COLS, ATT_Q_DIM, 2 * ATT_KV_DIM, FOURIER_DIM, n_g])
    wb = w_in.astype(BF16)
    ws = [wb[:, o[i]:o[i + 1]] for i in range(5)]
    nb = ROW_BATCH
    has_vres = v_first_src is not None
    row = lambda bb, i: (bb, i, 0)
    consts = [g.reshape(1, d)] + ws + [dft_c]
    feat_consts = [p["mu"], p["vec"], p["w0"], p["a0"], p["w2"], p["a2"], p["g2"], p["seg"]]
    in_specs, args, first = [], [], 0
    for x in parts:
        in_specs += _tile_with_halo_specs(nb, tm, d, x.shape[1] // tm, first)
        args += [x, x, x]
        first += x.shape[1] // tm
    in_specs += ([pl.BlockSpec((nb, 1, 6, d), lambda bb, i: (bb, jnp.where(i < n_lat_tiles, 0, 1), 0, 0))]
                 + [_const_spec(x.shape) for x in consts]
                 + [pl.BlockSpec((tm, ATT_Q_DIM), lambda bb, i: (i, 0)),
                    pl.BlockSpec((tm, ATT_Q_DIM), lambda bb, i: (i, 0))]
                 + [_const_spec(x.shape) for x in feat_consts])
    args += [modtab] + consts + [cos_t, sin_t] + feat_consts
    if has_vres:
        vres_consts = [p["v0"], p["v1"], p["v2"]]
        in_specs += [pl.BlockSpec((nb, tm, n), lambda bb, i: (bb, i, 1))] + [_const_spec(x.shape) for x in vres_consts]
        args += [v_first_src] + vres_consts
    outs = [(ATT_Q_DIM, BF16), (KV_COLS, BF16), (2 * FOURIER_DIM, BF16), (n_g, BF16),
            (3 * n, F32), (3 * n, F32), (3 * n, F32), (2 * n, F32)]
    return pl.pallas_call(
        functools.partial(_inproj_kernel, split_input=split_input, has_vres=has_vres, n_lat_tiles=n_lat_tiles,
                          n_tiles=n_tiles),
        grid=(b // nb, n_tiles),
        in_specs=in_specs,
        out_specs=[pl.BlockSpec((nb, tm, w), row) for w, _ in outs],
        out_shape=[jax.ShapeDtypeStruct((b, l, w), dt) for w, dt in outs],
        scratch_shapes=[pltpu.VMEM((nb, tm + 2 * HALO, RWKV_COLS), F32)],
        compiler_params=_cparams(2),
        name="in_projection",
    )(*args)


def _rwkv_features(buf_ref, tm, vf_ref, vres, mu_ref, vec_ref, w0_ref, a0_ref, w2_ref, a2_ref, g2_ref, seg_ref,
                   fc_ref, ff_ref, fb_ref, ro_ref):
    n = RWKV_DIM
    k_k = vec_ref[0:1, :]
    k_a = vec_ref[1:2, :]
    r_k = vec_ref[2:3, :]
    seg = seg_ref[...]
    tiles = []
    for bi in range(buf_ref.shape[0]):
        u = buf_ref[bi, HALO:HALO + tm, :]
        u_prev = buf_ref[bi, HALO - 1:HALO - 1 + tm, :]
        u_next = buf_ref[bi, HALO + 1:HALO + 1 + tm, :]
        us = u + mu_ref[0:1, :] * (u_prev - u) + mu_ref[1:2, :] * (u_next - u)
        o = 3 * n
        t = dict(bi=bi, r=us[:, 0:n], k=us[:, n:2 * n], v=us[:, 2 * n:3 * n])
        t["tw"] = [jnp.tanh(us[:, o + d * DECAY_LORA:o + (d + 1) * DECAY_LORA]).astype(BF16) for d in range(2)]
        o += 2 * DECAY_LORA
        t["xa"] = [us[:, o + d * AAA_LORA:o + (d + 1) * AAA_LORA].astype(BF16) for d in range(2)]
        o += 2 * AAA_LORA
        t["sg"] = _sigmoid(us[:, o:o + GATE_LORA]).astype(BF16)
        t["kk"] = t["k"] * k_k
        t["kk_sq"] = _split3(t["kk"] * t["kk"])[:2]
        tiles.append(t)
    for t in tiles:
        t["z"] = [jnp.dot(t["tw"][d], w2_ref[d], preferred_element_type=F32) for d in range(2)]
        t["za"] = [jnp.dot(t["xa"][d], a2_ref[d], preferred_element_type=F32) for d in range(2)]
        t["g"] = jnp.dot(t["sg"], g2_ref[...], preferred_element_type=F32)
        t["ss"] = sum(jnp.dot(part, seg, preferred_element_type=F32) for part in t["kk_sq"])
        if vres is not None:
            t["vv"] = _dot(t["v"], vres[1][...])
    if vres is not None:
        for t in tiles:
            t["vg"] = _dot(t["vv"], vres[2][...])
    for t in tiles:
        bi = t["bi"]
        v = t["v"]
        if vres is not None:
            v = v + (vf_ref[bi] - v) * _sigmoid(vres[0][...] + t["vg"])
        kk = t["kk"] * lax.rsqrt(jnp.maximum(t["ss"], 1e-24))
        keys = []
        for d, out_ref in enumerate((ff_ref, fb_ref)):
            z = w0_ref[d:d + 1, :] + t["z"][d]
            softplus = jnp.maximum(-z, 0.0) + jnp.log(1.0 + jnp.exp(-jnp.abs(z)))
            out_ref[bi, :, 0:n] = -jnp.exp(-softplus - 0.5)
            a = _sigmoid(a0_ref[d:d + 1, :] + t["za"][d])
            key = t["k"] * (1.0 + (a - 1.0) * k_a)
            keys.append(key)
            out_ref[bi, :, n:2 * n] = key
            out_ref[bi, :, 2 * n:3 * n] = kk * a
        t["v"] = v
        t["bonus_in"] = _split3(t["r"] * (0.5 * (keys[0] + keys[1])) * r_k)[:2]
        fc_ref[bi, :, 0:n] = t["r"]
        fc_ref[bi, :, n:2 * n] = v
        fc_ref[bi, :, 2 * n:3 * n] = kk
        ro_ref[bi, :, n:2 * n] = t["g"]
    for t in tiles:
        bonus = sum(jnp.dot(part, seg, preferred_element_type=F32) for part in t["bonus_in"])
        ro_ref[t["bi"], :, 0:n] = bonus * t["v"]


def _wkv_masks(c, reverse):
    ti = lax.broadcasted_iota(jnp.int32, (c, c), 0)
    si = lax.broadcasted_iota(jnp.int32, (c, c), 1)
    incl, strict = (si >= ti, si > ti) if reverse else (si <= ti, si < ti)
    levels = []
    for sh in range(int(math.log2(c))):
        bt = lax.shift_right_logical(ti, sh)
        bs = lax.shift_right_logical(si, sh)
        if reverse:
            levels.append(jnp.logical_and((bt & 1) == 0, bs == bt + 1))
        else:
            levels.append(jnp.logical_and((bt & 1) == 1, bs == bt - 1))
    return incl, strict, levels, jnp.where(ti == si, 1.0, 0.0)


def _wkv_kernel(fcf_ref, ff_ref, fcb_ref, fb_ref, yf_ref, yb_ref, sf_ref, sb_ref):
    @pl.when(pl.program_id(1) == 0)
    def _():
        sf_ref[...] = jnp.zeros_like(sf_ref)
        sb_ref[...] = jnp.zeros_like(sb_ref)

    c = WKV_CHUNK
    n = RWKV_DIM
    hd = RWKV_HEAD
    nb = fcf_ref.shape[0]
    probs = []
    for fc_ref, fd_ref, y_ref, s_ref, reverse in ((fcf_ref, ff_ref, yf_ref, sf_ref, False),
                                                  (fcb_ref, fb_ref, yb_ref, sb_ref, True)):
        incl, strict, levels, eye = _wkv_masks(c, reverse)
        tri = jnp.where(incl, 1.0, 0.0).astype(BF16)
        cums = [_dot_exact_lhs(tri, fd_ref[bi, :, 0:n]) for bi in range(nb)]
        for bi in range(nb):
            cum = cums[bi]
            total = cum[0:1, :] if reverse else cum[c - 1:c, :]
            c0 = 0.5 * total
            e_neg = jnp.exp(c0 - cum)
            a_t = -fc_ref[bi, :, 2 * n:3 * n] * jnp.exp(cum - fd_ref[bi, :, 0:n] - c0)
            r_t = fc_ref[bi, :, 0:n] * jnp.exp(cum - c0)
            b_t = fd_ref[bi, :, 2 * n:3 * n] * e_neg
            k_t = fd_ref[bi, :, n:2 * n] * e_neg
            e_half = jnp.exp(c0)
            e_tot = jnp.exp(total)
            for h in range(RWKV_HEADS):
                sl = slice(h * hd, (h + 1) * hd)
                probs.append(dict(
                    lhs=jnp.concatenate([a_t[:, sl], r_t[:, sl]], axis=0).astype(BF16),
                    rhs=jnp.concatenate([b_t[:, sl], k_t[:, sl]], axis=0).astype(BF16),
                    v=fc_ref[bi, :, n + h * hd:n + (h + 1) * hd], s0=s_ref[bi, h],
                    e_half=e_half[:, sl], e_tot=e_tot[:, sl], incl=incl, strict=strict, levels=levels, eye=eye,
                    y_ref=y_ref, s_ref=s_ref, bi=bi, h=h, sl=sl))

    for p in probs:
        p["g"] = _dot_nt(p["lhs"], p["rhs"])
    for p in probs:
        p["a_s"] = _dot_nt(p["lhs"], p["s0"] * p["e_half"])
    for p in probs:
        g = p["g"]
        p["a_ab"] = jnp.where(p["strict"], g[:c, :c], 0.0)
        a_ak = jnp.where(p["strict"], g[:c, c:], 0.0)
        p["a_r"] = jnp.concatenate([jnp.where(p["incl"], g[c:, :c], 0.0),
                                    jnp.where(p["incl"], g[c:, c:], 0.0)], axis=1).astype(BF16)
        p["t"] = p["eye"] + jnp.where(p["levels"][0], p["a_ab"], 0.0)
        p["rhs_u"] = p["a_s"][:c] + _dot(a_ak, p["v"])
    for lvl in range(1, len(probs[0]["levels"])):
        for p in probs:
            p["tb"] = p["t"].astype(BF16)
            p["tmp"] = _dot(jnp.where(p["levels"][lvl], p["a_ab"], 0.0), p["tb"])
        for p in probs:
            p["t"] = p["t"] + _dot(p["tb"], p["tmp"])
    for p in probs:
        p["uv"] = jnp.concatenate([_dot(p["t"], p["rhs_u"]), p["v"]], axis=0).astype(BF16)
    for p in probs:
        p["y_ref"][p["bi"], :, p["sl"]] = p["a_s"][c:] + _dot(p["a_r"], p["uv"])
    for p in probs:
        upd = lax.dot_general(p["uv"], p["rhs"], (((0,), (0,)), ((), ())), preferred_element_type=F32)
        p["s_ref"][p["bi"], p["h"]] = p["s0"] * p["e_tot"] + upd * p["e_half"]


def _wkv_scan(fc, ff, fb, n_lat):
    b, l, w = fc.shape
    c = WKV_CHUNK
    nb = WKV_BATCH
    nl = n_lat // c
    nc = l // c - nl
    fwd = lambda bb, j: (bb, jnp.where(j < nc, nl + j, j - nc), 0)
    bwd = lambda bb, j: (bb, nl + nc - 1 - j, 0)
    return pl.pallas_call(
        _wkv_kernel,
        grid=(b // nb, nl + nc),
        in_specs=[pl.BlockSpec((nb, c, w), fwd), pl.BlockSpec((nb, c, w), fwd),
                  pl.BlockSpec((nb, c, w), bwd), pl.BlockSpec((nb, c, w), bwd)],
        out_specs=[pl.BlockSpec((nb, c, RWKV_DIM), fwd), pl.BlockSpec((nb, c, RWKV_DIM), bwd)],
        out_shape=[jax.ShapeDtypeStruct((b, l, RWKV_DIM), F32)] * 2,
        scratch_shapes=[pltpu.VMEM((nb, RWKV_HEADS, RWKV_HEAD, RWKV_HEAD), F32)] * 2,
        compiler_params=_cparams(2),
        name="wkv_scan",
    )(fc, ff, fc, fb)


def _attn_kernel(sink_ref, q_ref, kv_ref, o_ref, *, n_lat):
    j = pl.program_id(1)
    qb = ATT_BLOCK
    hd = ATT_HEAD
    n_ctx = kv_ref.shape[1] - n_lat
    n_win = qb + 2 * WINDOW
    is_lat = j * qb < n_lat
    ws = pl.multiple_of(jnp.clip(j * qb - WINDOW, 0, n_lat - n_win), WINDOW)
    col = lax.broadcasted_iota(jnp.int32, (qb, n_ctx + n_win), 1)
    q_pos = j * qb + lax.broadcasted_iota(jnp.int32, (qb, n_ctx + n_win), 0)
    k_pos = ws + col - n_ctx
    valid = jnp.logical_or(col < n_ctx, jnp.logical_and(jnp.abs(q_pos - k_pos) <= WINDOW, is_lat))
    q = q_ref[0]
    kv = jnp.concatenate([kv_ref[0, n_lat:n_lat + n_ctx, :], kv_ref[0, pl.ds(ws, n_win), :]], axis=0)
    heads = range(ATT_HEADS)
    s = [jnp.where(valid, _dot_nt(q[:, h * hd:(h + 1) * hd],
                                  kv[:, (h // ATT_GROUP) * hd:(h // ATT_GROUP + 1) * hd]), NEG_INF) for h in heads]
    m = [jnp.maximum(jnp.max(s[h], axis=-1, keepdims=True), sink_ref[h]) for h in heads]
    p = [jnp.exp((s[h] - m[h]).astype(BF16)) for h in heads]
    o = [jnp.dot(p[h], kv[:, ATT_KV_DIM + (h // ATT_GROUP) * V_BLOCK:ATT_KV_DIM + (h // ATT_GROUP + 1) * V_BLOCK],
                 preferred_element_type=F32) for h in heads]
    for h in heads:
        den = pltpu.roll(o[h], hd, 1) + jnp.exp(sink_ref[h] - m[h])
        o_ref[0, :, h * hd:(h + 1) * hd] = (o[h] / den)[:, :hd].astype(o_ref.dtype)


def _attention(q, kv, sink, n_lat, n_rows):
    b, l, _ = q.shape
    qb = ATT_BLOCK
    return pl.pallas_call(
        functools.partial(_attn_kernel, n_lat=n_lat),
        grid=(b, n_rows // qb),
        in_specs=[pl.BlockSpec(memory_space=pltpu.SMEM),
                  pl.BlockSpec((1, qb, ATT_Q_DIM), lambda bb, j: (bb, j, 0)),
                  pl.BlockSpec((1, l, kv.shape[2]), lambda bb, j: (bb, 0, 0))],
        out_specs=pl.BlockSpec((1, qb, ATT_Q_DIM), lambda bb, j: (bb, j, 0)),
        out_shape=jax.ShapeDtypeStruct((b, n_rows, ATT_Q_DIM), BF16),
        compiler_params=_cparams(2),
        name="windowed_attention",
    )(sink, q, kv)


def _dft_kernel(z_ref, ct_ref, st_ref, o_ref):
    n = FOURIER_DIM
    o_ref[0] = (jnp.dot(ct_ref[...], z_ref[0, :, 0:n], preferred_element_type=F32)
                - jnp.dot(st_ref[...], z_ref[0, :, n:2 * n], preferred_element_type=F32)).astype(o_ref.dtype)


def _token_dft(z, ct, st, seg_rows, seg_block):
    b = z.shape[0]
    return pl.pallas_call(
        _dft_kernel,
        grid=(b,),
        in_specs=[pl.BlockSpec((1, seg_rows, z.shape[2]), lambda bb: (bb, seg_block, 0)),
                  _const_spec(ct.shape), _const_spec(st.shape)],
        out_specs=pl.BlockSpec((1, seg_rows, FOURIER_DIM), lambda bb: (bb, 0, 0)),
        out_shape=jax.ShapeDtypeStruct((b, seg_rows, FOURIER_DIM), BF16),
        compiler_params=_cparams(1),
        name="token_dft",
    )(z, ct, st)


def _merge_kernel(*refs, split_input, n_lat_tiles):
    if split_input:
        is_lat = pl.program_id(1) < n_lat_tiles
        x_ref, c_ref = refs[0:2]
        residual = lambda bi: jnp.where(is_lat, x_ref[bi], c_ref[bi])
        refs = refs[1:]
    else:
        residual = lambda bi: h_ref[bi]
    (h_ref, mod_ref, yf_ref, yb_ref, ro_ref, ya_ref, yd_ref, ug_ref, ln_ref, avg_ref,
     wbr_ref, wba_ref, wbf_ref, wo_ref, gp_ref, o_ref) = refs
    n = RWKV_DIM
    d = h_ref.shape[2]
    avg = avg_ref[...]
    nb, tm, _ = h_ref.shape
    rows = nb * tm
    y = (yf_ref[...] + yb_ref[...]).reshape(rows, n)
    mean = _dot_exact_rhs(y, avg)
    mix = (_sigmoid(ug_ref[:, :, d:2 * d].reshape(rows, d).astype(F32))
           * jnp.dot(ya_ref[...].reshape(rows, -1), wba_ref[...], preferred_element_type=F32))
    dev = y - mean
    var = _dot_exact_rhs(dev * dev, avg)
    mix += (_sigmoid(ug_ref[:, :, 2 * d:3 * d].reshape(rows, d).astype(F32))
            * jnp.dot(yd_ref[...].reshape(rows, -1), wbf_ref[...], preferred_element_type=F32))
    yn = dev * lax.rsqrt(var + LNX_EPS) * ln_ref[0:1, :] + ln_ref[1:2, :]
    ro = ro_ref[...].reshape(rows, 2 * n)
    y_r = (yn + ro[:, 0:n]) * ro[:, n:2 * n]
    mix += _sigmoid(ug_ref[:, :, 0:d].reshape(rows, d).astype(F32)) * _dot(y_r, wbr_ref[...])
    o = _dot(mix, wo_ref[...])
    o = o * lax.rsqrt(jnp.mean(o * o, axis=-1, keepdims=True) + EPS) * gp_ref[...]
    for bi in range(nb):
        o_ref[bi] = residual(bi) + mod_ref[bi, 0, 2:3, :] * o[bi * tm:(bi + 1) * tm]


def _merge(h, modtab, yf, yb, ro, ya, yd, ug, p, n_rows, n_lat_tiles):
    split_input = isinstance(h, tuple)
    parts = h if split_input else (h,)
    b, _, d = parts[0].shape
    tm = ROW_TILE
    nb = ROW_BATCH
    row = lambda bb, i: (bb, i, 0)
    consts = [p["ln"], p["avg"], p["wbr"], p["wba"], p["wbf"], p["wo"], p["gpost"]]
    h_specs, first = [], 0
    for x in parts:
        h_specs.append(_tile_with_halo_specs(nb, tm, d, x.shape[1] // tm, first)[0])
        first += x.shape[1] // tm
    return pl.pallas_call(
        functools.partial(_merge_kernel, split_input=split_input, n_lat_tiles=n_lat_tiles),
        grid=(b // nb, n_rows // tm),
        in_specs=h_specs + [
                  pl.BlockSpec((nb, 1, 6, d), lambda bb, i: (bb, jnp.where(i < n_lat_tiles, 0, 1), 0, 0)),
                  pl.BlockSpec((nb, tm, RWKV_DIM), row), pl.BlockSpec((nb, tm, RWKV_DIM), row),
                  pl.BlockSpec((nb, tm, 2 * RWKV_DIM), row),
                  pl.BlockSpec((nb, tm, ATT_Q_DIM), row), pl.BlockSpec((nb, tm, FOURIER_DIM), row),
                  pl.BlockSpec((nb, tm, 3 * d), row)] + [_const_spec(x.shape) for x in consts],
        out_specs=pl.BlockSpec((nb, tm, d), row),
        out_shape=jax.ShapeDtypeStruct((b, n_rows, d), F32),
        compiler_params=_cparams(2),
        name="branch_merge",
    )(*parts, modtab, yf, yb, ro, ya, yd, ug, *consts)


def _ffn_kernel(h_ref, hp_ref, hn_ref, mod_ref, gpre_ref, upg_ref, upv_ref, cw_ref, dn_ref, gpost_ref, o_ref,
                zg_ref, act_ref, *, n_lat_tiles, n_tiles):
    i = pl.program_id(1)
    tm = h_ref.shape[1]
    prev_ok = jnp.logical_and(i != 0, i != n_lat_tiles)
    next_ok = jnp.logical_and(i != n_lat_tiles - 1, i != n_tiles - 1)
    rid = lax.broadcasted_iota(jnp.int32, (tm + 2 * HALO, 1), 0)
    live = jnp.logical_and(jnp.logical_or(rid >= HALO, prev_ok), jnp.logical_or(rid < HALO + tm, next_ok))
    nb = h_ref.shape[0]
    th = tm + 2 * HALO
    f_all, f_main = [], []
    for bi in range(nb):
        x = jnp.concatenate([hp_ref[bi], h_ref[bi], hn_ref[bi]], axis=0)
        f = _rms_mod(x, gpre_ref[...], mod_ref[bi, 0, 3:4, :], mod_ref[bi, 0, 4:5, :])
        f_all.append(jnp.where(live, f, 0.0).astype(BF16))
        f_main.append(f[HALO:HALO + tm].astype(BF16))
    f_all = jnp.concatenate(f_all, axis=0)
    f_main = jnp.concatenate(f_main, axis=0)
    for c in range(zg_ref.shape[2] // FF_CHUNK):
        cols = slice(c * FF_CHUNK, (c + 1) * FF_CHUNK)
        zg_ref[:, :, cols] = jnp.dot(f_all, upg_ref[:, cols], preferred_element_type=F32).reshape(nb, th, -1)
        zv = jnp.dot(f_main, upv_ref[:, cols], preferred_element_type=F32).reshape(nb, tm, -1)
        zg = (cw_ref[0:1, cols] * zg_ref[:, HALO - 1:HALO - 1 + tm, cols]
              + cw_ref[1:2, cols] * zg_ref[:, HALO:HALO + tm, cols]
              + cw_ref[2:3, cols] * zg_ref[:, HALO + 1:HALO + 1 + tm, cols] + cw_ref[3:4, cols])
        act = 0.5 * zg * (1.0 + jnp.tanh(0.7978845608028654 * (zg + 0.044715 * zg * zg * zg)))
        act_ref[:, :, cols] = (act * zv).astype(BF16)
    o = jnp.dot(act_ref[...].reshape(nb * tm, -1), dn_ref[...], preferred_element_type=F32)
    o = o * lax.rsqrt(jnp.mean(o * o, axis=-1, keepdims=True) + EPS) * gpost_ref[...]
    for bi in range(nb):
        o_ref[bi] = h_ref[bi] + mod_ref[bi, 0, 5:6, :] * o[bi * tm:(bi + 1) * tm]


def _ffn(h, modtab, p, n_rows, n_lat_tiles):
    b, l, d = h.shape
    tm = ROW_TILE
    hb = tm // HALO
    n_tiles = l // tm
    d_ff = p["dn"].shape[0]
    nb = ROW_BATCH
    row = lambda bb, i: (bb, i, 0)
    consts = [p["gpre"], p["upg"], p["upv"], p["cw"], p["dn"], p["gpost"]]
    return pl.pallas_call(
        functools.partial(_ffn_kernel, n_lat_tiles=n_lat_tiles, n_tiles=n_tiles),
        grid=(b // nb, n_rows // tm),
        in_specs=[pl.BlockSpec((nb, tm, d), row),
                  pl.BlockSpec((nb, HALO, d), lambda bb, i: (bb, jnp.maximum(i * hb - 1, 0), 0)),
                  pl.BlockSpec((nb, HALO, d), lambda bb, i: (bb, jnp.minimum((i + 1) * hb, l // HALO - 1), 0)),
                  pl.BlockSpec((nb, 1, 6, d), lambda bb, i: (bb, jnp.where(i < n_lat_tiles, 0, 1), 0, 0))]
        + [_const_spec(x.shape) for x in consts],
        out_specs=pl.BlockSpec((nb, tm, d), row),
        out_shape=jax.ShapeDtypeStruct((b, n_rows, d), F32),
        scratch_shapes=[pltpu.VMEM((nb, tm + 2 * HALO, d_ff), F32), pltpu.VMEM((nb, tm, d_ff), BF16)],
        compiler_params=_cparams(2),
        name="conv_ffn",
    )(h, h, h, modtab, *consts)


def _rope_tables(n_lat, n_ctx):
    t = jnp.arange(n_lat)
    row_id = (t // GRID_W).astype(F32)
    col_id = (t % GRID_W).astype(F32)
    inv = ROPE_BASE ** (-jnp.arange(ROPE_FREQS, dtype=F32) / ROPE_FREQS)
    d = np.arange(ATT_HEAD)
    freq = d % ROPE_FREQS
    ang = jnp.where((d // (2 * ROPE_FREQS) == 0)[None, :], row_id[:, None], col_id[:, None]) * inv[freq][None, :]
    sign = np.where((d // ROPE_FREQS) % 2 == 0, -1.0, 1.0).astype(np.float32)
    cos = jnp.concatenate([jnp.cos(ang), jnp.ones((n_ctx, ATT_HEAD), F32)], axis=0)
    sin = jnp.concatenate([jnp.sin(ang) * sign[None, :], jnp.zeros((n_ctx, ATT_HEAD), F32)], axis=0)
    return jnp.tile(cos, (1, ATT_HEADS)), jnp.tile(sin, (1, ATT_HEADS))


def _dft_mats(n):
    r = 1 << (int(math.log2(n)) // 2)
    u = jnp.arange(n, dtype=jnp.int32)[None, :]

    def table(t):
        ang = ((t[:, None] * u) % n).astype(F32) * (2.0 * math.pi / n)
        return jnp.cos(ang), jnp.sin(ang)

    c_hi, s_hi = table(jnp.arange(n // r, dtype=jnp.int32) * r)
    c_lo, s_lo = table(jnp.arange(r, dtype=jnp.int32))
    scale = 1.0 / math.sqrt(n)
    c_hi, s_hi = c_hi[:, None, :] * scale, s_hi[:, None, :] * scale
    cos = (c_hi * c_lo[None] - s_hi * s_lo[None]).reshape(n, n)
    sin = (s_hi * c_lo[None] + c_hi * s_lo[None]).reshape(n, n)
    return cos, sin


def _block_diag(m, groups):
    return jnp.kron(jnp.eye(groups, dtype=m.dtype), m)


def kernel(x, c, ctx, c_ctx, mod_w, mod_b, norm_mix_pre, norm_mix_post, norm_ffn_pre, norm_ffn_post, w_in, rwkv_mu, rwkv_w0, rwkv_w2, rwkv_a0, rwkv_a2, rwkv_g2, rwkv_k_k, rwkv_k_a, rwkv_r_k, rwkv_lnx_w, rwkv_lnx_b, rwkv_v0, rwkv_v1, rwkv_v2, attn_sink, w_branch_rwkv, w_branch_attn, w_branch_fourier, w_out, ffn_up, ffn_conv_w, ffn_conv_b, ffn_down):
    b, n_lat, d = x.shape
    n_ctx = ctx.shape[1]
    depth = mod_w.shape[0]
    d_ff = ffn_down.shape[1]
    l = n_lat + n_ctx
    tm = ROW_TILE
    assert n_lat % tm == 0 and n_ctx % tm == 0 and n_lat % n_ctx == 0
    assert n_lat >= ATT_BLOCK + 2 * WINDOW and n_lat % ATT_BLOCK == 0 and n_ctx % ATT_BLOCK == 0
    assert ATT_BLOCK % WINDOW == 0 and d_ff % FF_CHUNK == 0 and b % WKV_BATCH == 0 and b % ROW_BATCH == 0
    n_lat_tiles = n_lat // tm

    cos_t, sin_t = _rope_tables(n_lat, n_ctx)
    cg, sg = _dft_mats(FOURIER_GROUP_DIM)
    dft_c = jnp.concatenate([_block_diag(cg, FOURIER_GROUPS), _block_diag(sg, FOURIER_GROUPS)], axis=1).astype(BF16)
    ct_lat, st_lat = (m.astype(BF16) for m in _dft_mats(n_lat))
    ct_ctx, st_ctx = (m.astype(BF16) for m in _dft_mats(n_ctx))
    seg = _block_diag(jnp.ones((RWKV_HEAD, RWKV_HEAD), F32), RWKV_HEADS).astype(BF16)
    avg = (seg.astype(F32) / RWKV_HEAD).astype(BF16)

    pad = (-(b + 1)) % 8
    cvec = jnp.concatenate([c, c_ctx[None, :], jnp.zeros((pad, d), F32)], axis=0)
    mod = _modulation(cvec, mod_w, mod_b)

    h = (x, ctx)
    v_first = None
    for layer in range(depth):
        last = layer == depth - 1
        lat = mod[layer, :b].reshape(b, 1, 6, d)
        cm = jnp.broadcast_to(mod[layer, b].reshape(1, 1, 6, d), (b, 1, 6, d))
        modtab = jnp.concatenate([lat, cm], axis=1)

        fp = {
            "mu": rwkv_mu[layer],
            "vec": jnp.stack([rwkv_k_k[layer], rwkv_k_a[layer], rwkv_r_k[layer]]),
            "w0": rwkv_w0[layer], "a0": rwkv_a0[layer],
            "w2": rwkv_w2[layer].astype(BF16), "a2": rwkv_a2[layer].astype(BF16),
            "g2": rwkv_g2[layer].astype(BF16), "seg": seg,
        }
        if layer > 0:
            lp = 128 - MV_LORA
            fp["v0"] = rwkv_v0[layer - 1].reshape(1, RWKV_DIM)
            fp["v1"] = jnp.pad(rwkv_v1[layer - 1], ((0, 0), (0, lp))).astype(BF16)
            fp["v2"] = jnp.pad(rwkv_v2[layer - 1], ((0, lp), (0, 0))).astype(BF16)
        q, kv, z, u_g, fc, ff, fb, ro = _inproj(h, modtab, norm_mix_pre[layer], w_in[layer], dft_c, cos_t, sin_t,
                                                v_first, fp, n_lat_tiles)
        if layer == 0:
            v_first = fc
        y_fwd, y_bwd = _wkv_scan(fc, ff, fb, n_lat)

        n_rows = n_lat if last else l
        y_att = _attention(q, kv, attn_sink[layer], n_lat, n_rows)
        y_dft = _token_dft(z, ct_lat, st_lat, n_lat, 0)
        if not last:
            y_dft = jnp.concatenate([y_dft, _token_dft(z, ct_ctx, st_ctx, n_ctx, n_lat // n_ctx)], axis=1)

        mp = {
            "ln": jnp.stack([rwkv_lnx_w[layer], rwkv_lnx_b[layer]]), "avg": avg,
            "wbr": w_branch_rwkv[layer].astype(BF16), "wba": w_branch_attn[layer].astype(BF16),
            "wbf": w_branch_fourier[layer].astype(BF16), "wo": w_out[layer].astype(BF16),
            "gpost": norm_mix_post[layer].reshape(1, d),
        }
        h = _merge(h, modtab, y_fwd, y_bwd, ro, y_att, y_dft, u_g, mp, n_rows, n_lat_tiles)

        up = ffn_up[layer].astype(BF16)
        pp = {
            "gpre": norm_ffn_pre[layer].reshape(1, d),
            "upg": up[:, :d_ff], "upv": up[:, d_ff:],
            "cw": jnp.concatenate([ffn_conv_w[layer], ffn_conv_b[layer][None, :]], axis=0),
            "dn": ffn_down[layer].astype(BF16),
            "gpost": norm_ffn_post[layer].reshape(1, d),
        }
        h = _ffn(h, modtab, pp, n_rows, n_lat_tiles)
    return h
```
